```python
import math
import jax
import jax.numpy as jnp
from jax import lax
import numpy as np

D_MODEL = 4096
BATCH = 16
SEQ = 256
DEPTH = 4
DEC_BATCH = 8
DEC_SEQ = 2048
PAST_LEN = 512

GRID_W = 64
N_MIXERS = 3
N_FNET = (DEPTH + 2) // 3
N_ATTN = (DEPTH + 1) // 3
N_SSD = DEPTH // 3
EPS = 1e-6
FNET_GROUPS = 8
N_HEADS = 32
N_KV = 8
Q_PER_KV = N_HEADS // N_KV
HEAD_DIM = 128
Q_DIM = N_HEADS * HEAD_DIM
KV_DIM = N_KV * HEAD_DIM
WINDOW = 128
BLOCK = 128
ROPE_THETA = 10000.0
ATTN_SCALE = HEAD_DIM ** -0.5
NEG_INF = -1e30
SSD_INNER = 2 * D_MODEL
SSD_HEADDIM = 64
SSD_HEADS = SSD_INNER // SSD_HEADDIM
SSD_GROUPS = 8
SSD_HPG = SSD_HEADS // SSD_GROUPS
SSD_STATE = 128
SSD_GN = SSD_GROUPS * SSD_STATE
SSD_CONV = 5
SSD_CONV_DIM = SSD_INNER + 2 * SSD_GN
SSD_IN_DIM = SSD_INNER + SSD_CONV_DIM + 2 * SSD_HEADS
SSD_CHUNK = 128
N_EXPERTS = 32
TOP_K = 4
D_EXPERT = D_MODEL // 4
SWIGLU_LIMIT = 7.0
SWIGLU_ALPHA = 1.702
MOE_BLOCK = 128

kernel_name = "hybrid_fnet_swa_ssd_moe_diffusion_step"


def rmsnorm(x, g):
    xf = x.astype(jnp.float32)
    y = xf * lax.rsqrt(jnp.mean(xf * xf, axis=-1, keepdims=True) + EPS)
    return (y * g.astype(jnp.float32)).astype(x.dtype)


def adaln(cond, w_ada, b_ada):
    mod = jax.nn.silu(cond) @ w_ada + b_ada
    return jnp.split(mod[:, None, :], 6, axis=-1)


def modulate(x, g, shift, scale):
    return rmsnorm(x, g) * (1 + scale) + shift


def fourier_mixer(h, w_o, b_o):
    b, L, d = h.shape
    hg = h.astype(jnp.float32).reshape(b, L, FNET_GROUPS, d // FNET_GROUPS)
    mixed = jnp.fft.fft2(hg, axes=(1, 3), norm='ortho').real
    return mixed.reshape(b, L, d).astype(h.dtype) @ w_o + b_o


def axial_rope(x, row_pos, col_pos):
    half = HEAD_DIM // 2
    inv_freq = ROPE_THETA ** (-jnp.arange(0, half, 2, dtype=jnp.float32) / half)
    bshape = (1, x.shape[1]) + (1,) * (x.ndim - 3) + (half // 2,)

    def rotate(xh, pos):
        ang = pos.astype(jnp.float32)[:, None] * inv_freq
        cos = jnp.cos(ang).reshape(bshape)
        sin = jnp.sin(ang).reshape(bshape)
        x1, x2 = jnp.split(xh.astype(jnp.float32), 2, axis=-1)
        return jnp.concatenate([x1 * cos - x2 * sin, x2 * cos + x1 * sin], axis=-1)

    out = jnp.concatenate([rotate(x[..., :half], row_pos), rotate(x[..., half:], col_pos)], axis=-1)
    return out.astype(x.dtype)


def attn_qkv(h, w_qkv, q_gain, k_gain):
    b, L, _ = h.shape
    q, k, v = jnp.split(h @ w_qkv, [Q_DIM, Q_DIM + KV_DIM], axis=-1)
    q = rmsnorm(q.reshape(b, L, N_KV, Q_PER_KV, HEAD_DIM), q_gain)
    k = rmsnorm(k.reshape(b, L, N_KV, HEAD_DIM), k_gain)
    v = v.reshape(b, L, N_KV, HEAD_DIM)
    return q, k, v


def _attend_block(qb, kb, vb, mask, sink):
    s = jnp.einsum('bqgrd,bkgd->bgrqk', qb, kb).astype(jnp.float32) * ATTN_SCALE
    if mask is not None:
        s = jnp.where(mask, s, NEG_INF)
    sk = jnp.broadcast_to(sink.astype(jnp.float32).reshape(1, N_KV, Q_PER_KV, 1, 1), s.shape[:-1] + (1,))
    p = jax.nn.softmax(jnp.concatenate([s, sk], axis=-1), axis=-1)[..., :-1]
    return jnp.einsum('bgrqk,bkgd->bqgrd', p.astype(vb.dtype), vb)


def ctx_attention(q, k, v, sink):
    b, L = q.shape[:2]

    def one(i):
        qb = lax.dynamic_slice_in_dim(q, i * BLOCK, BLOCK, axis=1)
        return _attend_block(qb, k, v, None, sink)

    o = lax.map(one, jnp.arange(L // BLOCK))
    return jnp.moveaxis(o, 0, 1).reshape(b, L, Q_DIM)


def latent_window_attention(q, k, v, kc, vc, sink):
    b, L = q.shape[:2]
    lc = kc.shape[1]
    pad = ((0, 0), (BLOCK, BLOCK), (0, 0), (0, 0))
    kp = jnp.pad(k, pad)
    vp = jnp.pad(v, pad)
    qi = jnp.arange(BLOCK)
    kj = jnp.arange(3 * BLOCK)
    ctx_mask = jnp.ones((BLOCK, lc), dtype=bool)

    def one(i):
        start = i * BLOCK
        qb = lax.dynamic_slice_in_dim(q, start, BLOCK, axis=1)
        kb = lax.dynamic_slice_in_dim(kp, start, 3 * BLOCK, axis=1)
        vb = lax.dynamic_slice_in_dim(vp, start, 3 * BLOCK, axis=1)
        qpos = start + qi
        kpos = start - BLOCK + kj
        band = (jnp.abs(qpos[:, None] - kpos[None, :]) <= WINDOW) & (kpos >= 0) & (kpos < L)
        mask = jnp.concatenate([band, ctx_mask], axis=1)
        return _attend_block(qb, jnp.concatenate([kb, kc], axis=1), jnp.concatenate([vb, vc], axis=1), mask, sink)

    o = lax.map(one, jnp.arange(L // BLOCK))
    return jnp.moveaxis(o, 0, 1).reshape(b, L, Q_DIM)


def centred_dwconv(x, w, bias):
    pad = SSD_CONV // 2
    y = lax.conv_general_dilated(x, w[:, None, :], window_strides=(1,), padding=[(pad, pad)],
                                 dimension_numbers=('NWC', 'WIO', 'NWC'), feature_group_count=x.shape[-1])
    return y + bias


def ssd_scan(x, dt, a, bm, cm, h0):
    b, L = x.shape[:2]
    nc = L // SSD_CHUNK

    def chunks(t):
        return jnp.moveaxis(t.reshape((b, nc, SSD_CHUNK) + t.shape[2:]), 1, 0)

    causal = jnp.tril(jnp.ones((SSD_CHUNK, SSD_CHUNK), dtype=bool))[None, :, :, None, None]

    def step(h, inp):
        xc, dtc, bc, cc = inp
        acum = jnp.cumsum(dtc * a, axis=1)
        lmat = jnp.exp(jnp.where(causal, acum[:, :, None] - acum[:, None], -jnp.inf))
        cb = jnp.einsum('bign,bjgn->bijg', cc, bc)
        y = jnp.einsum('bijg,bijgr,bjgr,bjgrp->bigrp', cb, lmat, dtc, xc)
        y = y + jnp.einsum('bign,bgrpn,bigr->bigrp', cc, h, jnp.exp(acum))
        a_last = acum[:, -1]
        h_new = h * jnp.exp(a_last)[..., None, None] + jnp.einsum(
            'bjgn,bjgr,bjgrp->bgrpn', bc, jnp.exp(a_last[:, None] - acum) * dtc, xc)
        return h_new.astype(h.dtype), y.astype(x.dtype)

    h_final, ys = lax.scan(step, h0, (chunks(x), chunks(dt), chunks(bm), chunks(cm)))
    return jnp.moveaxis(ys, 0, 1).reshape(x.shape), h_final


def gated_group_rmsnorm(y, z, g):
    b, L, _ = y.shape
    u = (y * jax.nn.silu(z)).astype(jnp.float32).reshape(b, L, SSD_GROUPS, SSD_INNER // SSD_GROUPS)
    u = u * lax.rsqrt(jnp.mean(u * u, axis=-1, keepdims=True) + EPS)
    return (u.reshape(b, L, SSD_INNER) * g.astype(jnp.float32)).astype(y.dtype)


def ssd_mixer(h, w_in, conv_w, conv_b, dt_bias, a_log, d_skip, norm_g, w_out, h0_f, h0_b):
    b, L, _ = h.shape
    z, xbc, dt = jnp.split(h @ w_in, [SSD_INNER, SSD_INNER + SSD_CONV_DIM], axis=-1)
    xbc = jax.nn.silu(centred_dwconv(xbc, conv_w, conv_b))
    xs, bm, cm = jnp.split(xbc, [SSD_INNER, SSD_INNER + SSD_GN], axis=-1)
    xs = xs.reshape(b, L, SSD_GROUPS, SSD_HPG, SSD_HEADDIM)
    bm = bm.reshape(b, L, SSD_GROUPS, SSD_STATE)
    cm = cm.reshape(b, L, SSD_GROUPS, SSD_STATE)
    dt = jax.nn.softplus((dt + dt_bias.reshape(-1)).astype(jnp.float32)).astype(h.dtype)
    dt = dt.reshape(b, L, 2, SSD_GROUPS, SSD_HPG)
    a = (-jnp.exp(a_log.astype(jnp.float32))).astype(h.dtype).reshape(2, SSD_GROUPS, SSD_HPG)
    y_f, h_f = ssd_scan(xs, dt[:, :, 0], a[0], bm, cm, h0_f)
    y_b, h_b = ssd_scan(jnp.flip(xs, 1), jnp.flip(dt[:, :, 1], 1), a[1], jnp.flip(bm, 1), jnp.flip(cm, 1), h0_b)
    y = y_f + jnp.flip(y_b, 1) + d_skip.reshape(SSD_GROUPS, SSD_HPG)[:, :, None] * xs
    y = gated_group_rmsnorm(y.reshape(b, L, SSD_INNER), z, norm_g)
    return y @ w_out, h_f, h_b


def moe_ffn(h, layer, router_w, router_b, w_gate_up, b_gate_up, w_down, b_down):
    bsz, L, d = h.shape
    x = h.reshape(-1, d)
    T = x.shape[0]
    logits = (x @ router_w[layer] + router_b[layer]).astype(jnp.float32)
    top_val, top_idx = lax.top_k(logits, TOP_K)
    gates = jax.nn.softmax(top_val, axis=-1)
    n_slots = T * TOP_K
    flat_e = top_idx.reshape(-1)
    flat_t = jnp.arange(n_slots, dtype=jnp.int32) // TOP_K
    order = jnp.argsort(flat_e)
    se, st, sg = flat_e[order], flat_t[order], gates.reshape(-1)[order]
    counts = jnp.zeros((N_EXPERTS,), jnp.int32).at[flat_e].add(1)
    padded = (counts + MOE_BLOCK - 1) // MOE_BLOCK * MOE_BLOCK
    pad_end = jnp.cumsum(padded)
    pad_start = pad_end - padded
    grp_start = jnp.cumsum(counts) - counts
    dest = pad_start[se] + jnp.arange(n_slots, dtype=jnp.int32) - grp_start[se]
    n_blocks = -(-n_slots // MOE_BLOCK) + N_EXPERTS
    cap = n_blocks * MOE_BLOCK
    slot_tok = jnp.full((cap,), T, jnp.int32).at[dest].set(st)
    slot_gate = jnp.zeros((cap,), jnp.float32).at[dest].set(sg)
    starts = jnp.arange(n_blocks, dtype=jnp.int32) * MOE_BLOCK
    block_exp = jnp.minimum(jnp.sum(starts[:, None] >= pad_end[None, :], axis=1), N_EXPERTS - 1)
    x_pad = jnp.concatenate([x, jnp.zeros((1, d), x.dtype)], axis=0)

    def expert_block(args):
        tok, e = args
        gu = x_pad[tok] @ w_gate_up[layer, e] + b_gate_up[layer, e]
        g, u = jnp.split(gu, 2, axis=-1)
        g = jnp.minimum(g, SWIGLU_LIMIT)
        u = jnp.clip(u, -SWIGLU_LIMIT, SWIGLU_LIMIT)
        act = (u + 1) * (g * jax.nn.sigmoid(SWIGLU_ALPHA * g))
        return act @ w_down[layer, e] + b_down[layer, e]

    y_blocks = lax.map(expert_block, (slot_tok.reshape(n_blocks, MOE_BLOCK), block_exp))
    y_slots = y_blocks.reshape(cap, d) * slot_gate[:, None].astype(y_blocks.dtype)
    y = jnp.zeros((T + 1, d), y_blocks.dtype).at[slot_tok].add(y_slots)
    return y[:T].reshape(bsz, L, d)


def setup_inputs(seed: int = 0) -> dict:
    key = jax.random.key(seed)
    keys = iter(jax.random.split(key, 48))

    def nrm(shape, scale):
        return scale * jax.random.normal(next(keys), shape, jnp.float32)

    def gain(shape):
        return 1.0 + nrm(shape, 0.02)

    dt = jnp.exp(jax.random.uniform(next(keys), (N_SSD, 2, SSD_HEADS), jnp.float32, math.log(1e-3), math.log(1e-1)))
    a_log = jnp.log(jax.random.uniform(next(keys), (N_SSD, 2, SSD_HEADS), jnp.float32, 1.0, 16.0))
    return {
        'x_prompt': nrm((BATCH, SEQ, D_MODEL), 1.0),
        'x_sample': nrm((DEC_BATCH, DEC_SEQ, D_MODEL), 1.0),
        'cache_attn_k': nrm((DEC_BATCH, N_ATTN, PAST_LEN, N_KV, HEAD_DIM), 1.0),
        'cache_attn_v': nrm((DEC_BATCH, N_ATTN, PAST_LEN, N_KV, HEAD_DIM), 1.0),
        'state_ssd_fwd': nrm((DEC_BATCH, N_SSD, SSD_HEADS, SSD_HEADDIM, SSD_STATE), 0.5),
        'state_ssd_bwd': nrm((DEC_BATCH, N_SSD, SSD_HEADS, SSD_HEADDIM, SSD_STATE), 0.5),
        'c': nrm((DEC_BATCH, D_MODEL), 1.0),
        'c_ctx': nrm((D_MODEL,), 1.0),
        'w_ada': nrm((DEPTH, D_MODEL, 6 * D_MODEL), 0.5 * D_MODEL ** -0.5),
        'b_ada': nrm((DEPTH, 6 * D_MODEL), 0.01),
        'norm1_g': gain((DEPTH, D_MODEL)),
        'norm2_g': gain((DEPTH, D_MODEL)),
        'fnet_w_out': nrm((N_FNET, D_MODEL, D_MODEL), D_MODEL ** -0.5),
        'fnet_b_out': nrm((N_FNET, D_MODEL), 0.01),
        'attn_w_qkv': nrm((N_ATTN, D_MODEL, Q_DIM + 2 * KV_DIM), D_MODEL ** -0.5),
        'attn_q_gain': gain((N_ATTN, HEAD_DIM)),
        'attn_k_gain': gain((N_ATTN, HEAD_DIM)),
        'attn_sink': nrm((N_ATTN, N_HEADS), 1.0),
        'attn_w_out': nrm((N_ATTN, Q_DIM, D_MODEL), Q_DIM ** -0.5),
        'ssd_w_in': nrm((N_SSD, D_MODEL, SSD_IN_DIM), D_MODEL ** -0.5),
        'ssd_conv_w': nrm((N_SSD, SSD_CONV, SSD_CONV_DIM), SSD_CONV ** -0.5),
        'ssd_conv_b': nrm((N_SSD, SSD_CONV_DIM), 0.01),
        'ssd_dt_bias': dt + jnp.log(-jnp.expm1(-dt)),
        'ssd_a_log': a_log,
        'ssd_d': 1.0 + nrm((N_SSD, SSD_HEADS), 0.1),
        'ssd_norm_g': gain((N_SSD, SSD_INNER)),
        'ssd_w_out': nrm((N_SSD, SSD_INNER, D_MODEL), SSD_INNER ** -0.5),
        'moe_router_w': nrm((DEPTH, D_MODEL, N_EXPERTS), D_MODEL ** -0.5),
        'moe_router_b': nrm((DEPTH, N_EXPERTS), 0.01),
        'moe_w_gate_up': nrm((DEPTH, N_EXPERTS, D_MODEL, 2 * D_EXPERT), D_MODEL ** -0.5),
        'moe_b_gate_up': nrm((DEPTH, N_EXPERTS, 2 * D_EXPERT), 0.01),
        'moe_w_down': nrm((DEPTH, N_EXPERTS, D_EXPERT, D_MODEL), D_EXPERT ** -0.5),
        'moe_b_down': nrm((DEPTH, N_EXPERTS, D_MODEL), 0.01),
    }


def reference(x_prompt, x_sample, cache_attn_k, cache_attn_v, state_ssd_fwd, state_ssd_bwd, c, c_ctx,
              w_ada, b_ada, norm1_g, norm2_g, fnet_w_out, fnet_b_out,
              attn_w_qkv, attn_q_gain, attn_k_gain, attn_sink, attn_w_out,
              ssd_w_in, ssd_conv_w, ssd_conv_b, ssd_dt_bias, ssd_a_log, ssd_d, ssd_norm_g, ssd_w_out,
              moe_router_w, moe_router_b, moe_w_gate_up, moe_b_gate_up, moe_w_down, moe_b_down):
    b_p = x_prompt.shape[0]
    b_s, s_s = x_sample.shape[:2]
    rows = s_s // GRID_W
    row_pos = jnp.repeat(jnp.arange(rows, dtype=jnp.int32), GRID_W)
    col_pos = jnp.tile(jnp.arange(GRID_W, dtype=jnp.int32), rows)
    moe_w = (moe_router_w, moe_router_b, moe_w_gate_up, moe_b_gate_up, moe_w_down, moe_b_down)
    xp, xs = x_prompt, x_sample
    new_k, new_v, new_f, new_b = [], [], [], []
    for l in range(DEPTH):
        kind, j = l % N_MIXERS, l // N_MIXERS
        shp1, scp1, gtp1, shp2, scp2, gtp2 = adaln(c_ctx[None, :], w_ada[l], b_ada[l])
        shs1, scs1, gts1, shs2, scs2, gts2 = adaln(c, w_ada[l], b_ada[l])
        hp = modulate(xp, norm1_g[l], shp1, scp1)
        hs = modulate(xs, norm1_g[l], shs1, scs1)
        if kind == 0:
            op = fourier_mixer(hp, fnet_w_out[j], fnet_b_out[j])
            os_ = fourier_mixer(hs, fnet_w_out[j], fnet_b_out[j])
        elif kind == 1:
            qp, kp, vp = attn_qkv(hp, attn_w_qkv[j], attn_q_gain[j], attn_k_gain[j])
            new_k.append(kp)
            new_v.append(vp)
            op = ctx_attention(qp, kp, vp, attn_sink[j]) @ attn_w_out[j]
            qs, ks, vs = attn_qkv(hs, attn_w_qkv[j], attn_q_gain[j], attn_k_gain[j])
            qs = axial_rope(qs, row_pos, col_pos)
            ks = axial_rope(ks, row_pos, col_pos)
            os_ = latent_window_attention(qs, ks, vs, cache_attn_k[:, j], cache_attn_v[:, j], attn_sink[j]) @ attn_w_out[j]
        else:
            ssd_w = (ssd_w_in[j], ssd_conv_w[j], ssd_conv_b[j], ssd_dt_bias[j], ssd_a_log[j], ssd_d[j], ssd_norm_g[j], ssd_w_out[j])
            h0 = jnp.zeros((b_p, SSD_GROUPS, SSD_HPG, SSD_HEADDIM, SSD_STATE), xp.dtype)
            op, hf, hb = ssd_mixer(hp, *ssd_w, h0, h0)
            new_f.append(hf.reshape(b_p, SSD_HEADS, SSD_HEADDIM, SSD_STATE))
            new_b.append(hb.reshape(b_p, SSD_HEADS, SSD_HEADDIM, SSD_STATE))
            s0_f = state_ssd_fwd[:, j].reshape(b_s, SSD_GROUPS, SSD_HPG, SSD_HEADDIM, SSD_STATE)
            s0_b = state_ssd_bwd[:, j].reshape(b_s, SSD_GROUPS, SSD_HPG, SSD_HEADDIM, SSD_STATE)
            os_, _, _ = ssd_mixer(hs, *ssd_w, s0_f, s0_b)
        xp = xp + gtp1 * op
        xs = xs + gts1 * os_
        xp = xp + gtp2 * moe_ffn(modulate(xp, norm2_g[l], shp2, scp2), l, *moe_w)
        xs = xs + gts2 * moe_ffn(modulate(xs, norm2_g[l], shs2, scs2), l, *moe_w)
    new_cache_attn_k = jnp.stack(new_k, axis=1)
    new_cache_attn_v = jnp.stack(new_v, axis=1)
    new_state_ssd_fwd = jnp.stack(new_f, axis=1)
    new_state_ssd_bwd = jnp.stack(new_b, axis=1)
    return (xp, xs, new_cache_attn_k, new_cache_attn_v, new_state_ssd_fwd, new_state_ssd_bwd)
```

```python
import functools
import math

import jax
import jax.numpy as jnp
from jax import lax
from jax.experimental import pallas as pl
from jax.experimental.pallas import tpu as pltpu

GRID_W = 64
N_MIXERS = 3
EPS = 1e-6
FNET_GROUPS = 8
N_HEADS = 32
N_KV = 8
HEAD_DIM = 128
WINDOW = 128
BLOCK = 128
ROPE_THETA = 10000.0
NEG_INF = -1e30
SSD_HEADDIM = 64
SSD_GROUPS = 8
SSD_STATE = 128
SSD_CONV = 5
SSD_CHUNK = 128
TOP_K = 4
SWIGLU_LIMIT = 7.0
SWIGLU_ALPHA = 1.702

V7X_LANES = 128
V7X_VMEM_LIMIT_BYTES = 56 * 1024 * 1024
COND_ROWS = 16
MOE_BLOCK_ROWS = 256

F32 = jnp.float32
BF16 = jnp.bfloat16


def _params(*sem):
    return pltpu.CompilerParams(dimension_semantics=sem, vmem_limit_bytes=V7X_VMEM_LIMIT_BYTES)


def _pick(dim, pref, align=V7X_LANES):
    if dim <= pref:
        return dim
    t = pref - pref % align
    while t >= align:
        if dim % t == 0:
            return t
        t -= align
    return dim


def _mm_body(*refs, nk, k_axis, silu_in, has_bias, has_res, groups):
    it = iter(refs)
    x_ref, w_ref = next(it), next(it)
    b_ref = next(it) if has_bias else None
    r_ref = next(it) if has_res else None
    g_ref = next(it) if has_res else None
    o_ref, acc_ref = next(it), next(it)
    k = pl.program_id(k_axis)

    @pl.when(k == 0)
    def _():
        acc_ref[...] = jnp.zeros_like(acc_ref)

    x = x_ref[...]
    if silu_in:
        x = x.astype(F32)
        x = x * jax.nn.sigmoid(x)
    acc_ref[...] += jnp.dot(x.astype(BF16), w_ref[...].astype(BF16), preferred_element_type=F32)

    @pl.when(k == nk - 1)
    def _():
        r = acc_ref[...]
        if has_bias:
            r = r + b_ref[...]
        if has_res:
            tm, tn = r.shape
            r = (r.reshape(groups, tm // groups, tn) * g_ref[...]).reshape(tm, tn) + r_ref[...]
        o_ref[...] = r.astype(o_ref.dtype)


def _matmul(x, w, *, w_lead=(), n_off=0, n_out=None, bias=None, res=None, gate=None, gate_col=0,
            out_dtype=F32, tm=1024, tn=1024, tk=512):
    m, kdim = x.shape
    n = w.shape[-1] if n_out is None else n_out
    tm, tn, tk = _pick(m, tm, 8), _pick(n, tn), _pick(kdim, tk)
    nk = kdim // tk
    lead = tuple(w_lead)
    in_specs = [
        pl.BlockSpec((tm, tk), lambda i, j, k: (i, k)),
        pl.BlockSpec((None,) * len(lead) + (tk, tn), lambda i, j, k: lead + (k, j + n_off)),
    ]
    args = [x, w]
    if bias is not None:
        in_specs.append(pl.BlockSpec((1, tn), lambda i, j, k: (0, j)))
        args.append(bias.reshape(1, n))
    groups = 1
    if res is not None:
        rows = m // gate.shape[0]
        assert tm % rows == 0
        groups = tm // rows
        ncol = n // tn
        in_specs.append(pl.BlockSpec((tm, tn), lambda i, j, k: (i, j)))
        in_specs.append(pl.BlockSpec((groups, 1, tn), lambda i, j, k: (i, 0, gate_col * ncol + j)))
        args += [res, gate]
    body = functools.partial(_mm_body, nk=nk, k_axis=2, silu_in=False, has_bias=bias is not None,
                             has_res=res is not None, groups=groups)
    return pl.pallas_call(
        body,
        grid=(m // tm, n // tn, nk),
        in_specs=in_specs,
        out_specs=pl.BlockSpec((tm, tn), lambda i, j, k: (i, j)),
        out_shape=jax.ShapeDtypeStruct((m, n), out_dtype),
        scratch_shapes=[pltpu.VMEM((tm, tn), F32)],
        compiler_params=_params("parallel", "parallel", "arbitrary"),
    )(*args)


def _matmul_wbatched(x, w, bias=None, *, silu_in=False, out_dtype=F32, tm=1024, tn=1024, tk=512):
    m, kdim = x.shape
    nb, _, n = w.shape
    tm, tn, tk = _pick(m, tm, 8), _pick(n, tn), _pick(kdim, tk)
    nk = kdim // tk
    in_specs = [
        pl.BlockSpec((tm, tk), lambda b, i, j, k: (i, k)),
        pl.BlockSpec((None, tk, tn), lambda b, i, j, k: (b, k, j)),
    ]
    args = [x, w]
    if bias is not None:
        in_specs.append(pl.BlockSpec((None, 1, tn), lambda b, i, j, k: (b, 0, j)))
        args.append(bias.reshape(nb, 1, n))
    body = functools.partial(_mm_body, nk=nk, k_axis=3, silu_in=silu_in, has_bias=bias is not None,
                             has_res=False, groups=1)
    return pl.pallas_call(
        body,
        grid=(nb, m // tm, n // tn, nk),
        in_specs=in_specs,
        out_specs=pl.BlockSpec((None, tm, tn), lambda b, i, j, k: (b, i, j)),
        out_shape=jax.ShapeDtypeStruct((nb, m, n), out_dtype),
        scratch_shapes=[pltpu.VMEM((tm, tn), F32)],
        compiler_params=_params("parallel", "parallel", "parallel", "arbitrary"),
    )(*args)


def _modulated(x, g_ref, sh_ref, sc_ref):
    y = x * lax.rsqrt(jnp.mean(x * x, axis=-1, keepdims=True) + EPS) * g_ref[...]
    return y * (1.0 + sc_ref[0]) + sh_ref[0]


def _modulate_body(x_ref, g_ref, sh_ref, sc_ref, o_ref):
    o_ref[...] = _modulated(x_ref[...], g_ref, sh_ref, sc_ref).astype(o_ref.dtype)


def _mod_specs(rows, d, col):
    return [
        pl.BlockSpec((rows, d), lambda i: (i, 0)),
        pl.BlockSpec((1, d), lambda i: (0, 0)),
        pl.BlockSpec((1, 1, d), lambda i: (i, 0, col)),
        pl.BlockSpec((1, 1, d), lambda i: (i, 0, col + 1)),
    ]


def _modulate(x, g, modrows, col):
    t, d = x.shape
    rows = t // modrows.shape[0]
    return pl.pallas_call(
        _modulate_body,
        grid=(t // rows,),
        in_specs=_mod_specs(rows, d, col),
        out_specs=pl.BlockSpec((rows, d), lambda i: (i, 0)),
        out_shape=jax.ShapeDtypeStruct((t, d), BF16),
        compiler_params=_params("parallel"),
    )(x, g.reshape(1, d), modrows, modrows)


def _modulate_router_body(x_ref, g_ref, sh_ref, sc_ref, rw_ref, rb_ref, h_ref, ti_ref, tg_ref):
    h = _modulated(x_ref[...], g_ref, sh_ref, sc_ref)
    h_ref[...] = h
    logits = jnp.dot(h, rw_ref[...], precision=lax.Precision.HIGHEST, preferred_element_type=F32) + rb_ref[...]
    lane = lax.broadcasted_iota(jnp.int32, logits.shape, 1).astype(F32)
    n_lane = float(logits.shape[1])
    vals, idxs = [], []
    for _ in range(TOP_K):
        m = jnp.max(logits, axis=-1, keepdims=True)
        idx = jnp.min(jnp.where(logits == m, lane, n_lane), axis=-1, keepdims=True)
        vals.append(m)
        idxs.append(idx)
        logits = jnp.where(lane == idx, -jnp.inf, logits)
    es = [jnp.exp(v - vals[0]) for v in vals]
    inv = 1.0 / functools.reduce(lambda a, b: a + b, es)
    ti = jnp.zeros(lane.shape, jnp.int32)
    tg = jnp.zeros(lane.shape, F32)
    for k in range(TOP_K):
        ti = jnp.where(lane == k, idxs[k].astype(jnp.int32), ti)
        tg = jnp.where(lane == k, es[k] * inv, tg)
    ti_ref[...] = ti
    tg_ref[...] = tg


def _modulate_router(x, g, modrows, col, rw, rb):
    t, d = x.shape
    rows = t // modrows.shape[0]
    e = rw.shape[1]
    rw_p = jnp.zeros((d, V7X_LANES), F32).at[:, :e].set(rw)
    rb_p = jnp.full((1, V7X_LANES), NEG_INF, F32).at[0, :e].set(rb)
    return pl.pallas_call(
        _modulate_router_body,
        grid=(t // rows,),
        in_specs=_mod_specs(rows, d, col) + [
            pl.BlockSpec((d, V7X_LANES), lambda i: (0, 0)),
            pl.BlockSpec((1, V7X_LANES), lambda i: (0, 0)),
        ],
        out_specs=[
            pl.BlockSpec((rows, d), lambda i: (i, 0)),
            pl.BlockSpec((rows, V7X_LANES), lambda i: (i, 0)),
            pl.BlockSpec((rows, V7X_LANES), lambda i: (i, 0)),
        ],
        out_shape=[
            jax.ShapeDtypeStruct((t, d), F32),
            jax.ShapeDtypeStruct((t, V7X_LANES), jnp.int32),
            jax.ShapeDtypeStruct((t, V7X_LANES), F32),
        ],
        compiler_params=_params("parallel"),
    )(x, g.reshape(1, d), modrows, modrows, rw_p, rb_p)


def _dft_tables(n):
    j = jnp.arange(n, dtype=jnp.int32)
    ang = ((j[:, None] * j[None, :]) % n).astype(F32) * (2.0 * math.pi / n)
    s = 1.0 / math.sqrt(n)
    return jnp.cos(ang) * s, jnp.sin(ang) * s


def _fnet_channel_dft(h, row_off, nb, seq, tm=512):
    d = h.shape[1]
    c = d // FNET_GROUPS
    cs = jnp.stack(_dft_tables(c))
    tm = _pick(seq, tm, 8)
    lt = seq // tm
    off = row_off // tm
    body = functools.partial(_mm_body, nk=1, k_axis=4, silu_in=False, has_bias=False, has_res=False, groups=1)
    return pl.pallas_call(
        body,
        grid=(nb, lt, FNET_GROUPS, 2, 1),
        in_specs=[
            pl.BlockSpec((tm, c), lambda b, i, g, s, k: (off + b * lt + i, g)),
            pl.BlockSpec((None, c, c), lambda b, i, g, s, k: (s, 0, 0)),
        ],
        out_specs=pl.BlockSpec((None, None, tm, c), lambda b, i, g, s, k: (b, s, i, g)),
        out_shape=jax.ShapeDtypeStruct((nb, 2, seq, d), BF16),
        scratch_shapes=[pltpu.VMEM((tm, c), F32)],
        compiler_params=_params("parallel", "parallel", "parallel", "parallel", "arbitrary"),
    )(h, cs)


def _fourier_mix(h, row_off, nb, seq):
    d = h.shape[1]
    ab = _fnet_channel_dft(h, row_off, nb, seq).reshape(nb, 2 * seq, d)
    cos_l, sin_l = _dft_tables(seq)
    csl = jnp.concatenate([cos_l, -sin_l], axis=1)
    return _matmul_wbatched(csl, ab, out_dtype=BF16).reshape(nb * seq, d)


def _rope_tables(seq, rows_identity):
    half = HEAD_DIM // 2
    inv_freq = ROPE_THETA ** (-jnp.arange(0, half, 2, dtype=F32) / half)
    rows = seq // GRID_W
    row_pos = jnp.repeat(jnp.arange(rows, dtype=jnp.int32), GRID_W).astype(F32)
    col_pos = jnp.tile(jnp.arange(GRID_W, dtype=jnp.int32), rows).astype(F32)
    ar = row_pos[:, None] * inv_freq
    ac = col_pos[:, None] * inv_freq
    cos_t = jnp.concatenate([jnp.cos(ar), jnp.cos(ar), jnp.cos(ac), jnp.cos(ac)], axis=-1)
    sin_t = jnp.concatenate([-jnp.sin(ar), jnp.sin(ar), -jnp.sin(ac), jnp.sin(ac)], axis=-1)
    cos_t = jnp.concatenate([cos_t, jnp.ones((rows_identity, HEAD_DIM), F32)], axis=0)
    sin_t = jnp.concatenate([sin_t, jnp.zeros((rows_identity, HEAD_DIM), F32)], axis=0)
    return cos_t, sin_t


def _qknorm_rope_body(x_ref, g_ref, cos_ref, sin_ref, o_ref):
    cos_t, sin_t, gain = cos_ref[...], sin_ref[...], g_ref[...]
    lane = lax.broadcasted_iota(jnp.int32, cos_t.shape, 1)
    first = (lane % (HEAD_DIM // 2)) < (HEAD_DIM // 4)
    for h in range(x_ref.shape[1] // HEAD_DIM):
        sl = slice(h * HEAD_DIM, (h + 1) * HEAD_DIM)
        x = x_ref[:, sl]
        y = x * lax.rsqrt(jnp.mean(x * x, axis=-1, keepdims=True) + EPS) * gain
        partner = jnp.where(first, pltpu.roll(y, HEAD_DIM - HEAD_DIM // 4, 1), pltpu.roll(y, HEAD_DIM // 4, 1))
        o_ref[:, sl] = (y * cos_t + partner * sin_t).astype(o_ref.dtype)


def _qknorm_rope(qkv, gain, cos_t, sin_t, col_off, n_cols, rows, tab_index, out_dtype, tc=1024):
    t = qkv.shape[0]
    tc = _pick(n_cols, tc)
    off = col_off // tc
    return pl.pallas_call(
        _qknorm_rope_body,
        grid=(t // rows, n_cols // tc),
        in_specs=[
            pl.BlockSpec((rows, tc), lambda i, j: (i, off + j)),
            pl.BlockSpec((1, HEAD_DIM), lambda i, j: (0, 0)),
            pl.BlockSpec((rows, HEAD_DIM), lambda i, j: (tab_index(i), 0)),
            pl.BlockSpec((rows, HEAD_DIM), lambda i, j: (tab_index(i), 0)),
        ],
        out_specs=pl.BlockSpec((rows, tc), lambda i, j: (i, j)),
        out_shape=jax.ShapeDtypeStruct((t, n_cols), out_dtype),
        compiler_params=_params("parallel", "parallel"),
    )(qkv, gain.reshape(1, HEAD_DIM), cos_t, sin_t)


def _attn_body(sink_ref, q_ref, *refs, n_win, seq_len):
    kw, vw = refs[:n_win], refs[n_win:2 * n_win]
    kc_ref, vc_ref, o_ref = refs[2 * n_win:]
    g, i = pl.program_id(1), pl.program_id(2)
    rep = q_ref.shape[1] // HEAD_DIM
    nq = q_ref.shape[0]
    q = jnp.concatenate([q_ref[:, r * HEAD_DIM:(r + 1) * HEAD_DIM] for r in range(rep)], axis=0).astype(BF16)
    keys = jnp.concatenate([r[...].astype(BF16) for r in kw] + [kc_ref[...].astype(BF16)], axis=0)
    vals = jnp.concatenate([r[...].astype(BF16) for r in vw] + [vc_ref[...].astype(BF16)], axis=0)
    s = lax.dot_general(q, keys, (((1,), (1,)), ((), ())), preferred_element_type=F32) * (HEAD_DIM ** -0.5)
    if n_win:
        qpos = i * BLOCK + lax.broadcasted_iota(jnp.int32, s.shape, 0) % nq
        kj = lax.broadcasted_iota(jnp.int32, s.shape, 1)
        kpos = i * BLOCK - BLOCK + kj
        band = (jnp.abs(qpos - kpos) <= WINDOW) & (kpos >= 0) & (kpos < seq_len)
        s = jnp.where(band | (kj >= n_win * BLOCK), s, NEG_INF)
    sink = jnp.concatenate([jnp.full((nq, 1), sink_ref[g * rep + r], F32) for r in range(rep)], axis=0)
    m = jnp.maximum(jnp.max(s, axis=-1, keepdims=True), sink)
    p = jnp.exp(s - m)
    inv = 1.0 / (jnp.sum(p, axis=-1, keepdims=True) + jnp.exp(sink - m))
    o = jnp.dot((p * inv).astype(BF16), vals, preferred_element_type=F32)
    for r in range(rep):
        o_ref[:, r * HEAD_DIM:(r + 1) * HEAD_DIM] = o[r * nq:(r + 1) * nq].astype(o_ref.dtype)


def _attention(qn, kn, qkv, v_col_off, sink, kc, vc, row_off, nb, seq, windowed):
    rep = N_HEADS // N_KV
    qw = rep * HEAD_DIM
    vo = v_col_off // HEAD_DIM
    if windowed:
        nblk = seq // BLOCK
        ro = row_off // BLOCK
        lc = kc.shape[1]

        def win(shift, col):
            return pl.BlockSpec(
                (BLOCK, HEAD_DIM),
                lambda b, g, i, s: (ro + b * nblk + jnp.clip(i + shift, 0, nblk - 1), col + g))

        in_specs = [pl.BlockSpec((BLOCK, qw), lambda b, g, i, s: (ro + b * nblk + i, g))]
        in_specs += [win(-1, 0), win(0, 0), win(1, 0), win(-1, vo), win(0, vo), win(1, vo)]
        in_specs += [pl.BlockSpec((None, lc, HEAD_DIM), lambda b, g, i, s: (b, 0, g))] * 2
        args = [qn, kn, kn, kn, qkv, qkv, qkv, kc, vc]
        grid = (nb, N_KV, nblk)
        out_spec = pl.BlockSpec((BLOCK, qw), lambda b, g, i, s: (b * nblk + i, g))
        body = functools.partial(_attn_body, n_win=3, seq_len=seq)
    else:
        ro = row_off // seq
        in_specs = [
            pl.BlockSpec((seq, qw), lambda b, g, i, s: (ro + b, g)),
            pl.BlockSpec((seq, HEAD_DIM), lambda b, g, i, s: (ro + b, g)),
            pl.BlockSpec((seq, HEAD_DIM), lambda b, g, i, s: (ro + b, vo + g)),
        ]
        args = [qn, kn, qkv]
        grid = (nb, N_KV, 1)
        out_spec = pl.BlockSpec((seq, qw), lambda b, g, i, s: (b, g))
        body = functools.partial(_attn_body, n_win=0, seq_len=seq)
    return pl.pallas_call(
        body,
        grid_spec=pltpu.PrefetchScalarGridSpec(
            num_scalar_prefetch=1, grid=grid, in_specs=in_specs, out_specs=out_spec),
        out_shape=jax.ShapeDtypeStruct((nb * seq, N_HEADS * HEAD_DIM), BF16),
        compiler_params=_params("parallel", "parallel", "parallel"),
    )(sink.astype(F32), *args)


def _conv_silu_body(x_ref, w_ref, b_ref, o_ref):
    x = x_ref[...]
    seq = x.shape[0]
    pad = SSD_CONV // 2
    t = lax.broadcasted_iota(jnp.int32, (seq, 1), 0)
    acc = b_ref[...] + x * w_ref[pad:pad + 1, :]
    for k in range(SSD_CONV):
        s = k - pad
        if s == 0:
            continue
        shifted = pltpu.roll(x, (-s) % seq, 0)
        acc = acc + jnp.where((t + s >= 0) & (t + s < seq), shifted, 0.0) * w_ref[k:k + 1, :]
    o_ref[...] = acc * jax.nn.sigmoid(acc)


def _conv_silu(zx, col_off, n_cols, w, b, row_off, nb, seq, tc=256):
    tc = _pick(n_cols, tc)
    co, ro = col_off // tc, row_off // seq
    return pl.pallas_call(
        _conv_silu_body,
        grid=(nb, n_cols // tc),
        in_specs=[
            pl.BlockSpec((seq, tc), lambda b, j: (ro + b, co + j)),
            pl.BlockSpec((SSD_CONV, tc), lambda b, j: (0, j)),
            pl.BlockSpec((1, tc), lambda b, j: (0, j)),
        ],
        out_specs=pl.BlockSpec((seq, tc), lambda b, j: (b, j)),
        out_shape=jax.ShapeDtypeStruct((nb * seq, n_cols), F32),
        compiler_params=_params("parallel", "parallel"),
    )(zx, w, b.reshape(1, n_cols))


def _softplus(x):
    return jnp.maximum(x, 0.0) + jnp.log(1.0 + jnp.exp(-jnp.abs(x)))


def _ssd_scan_body(x_ref, b_ref, c_ref, dt_ref, dtt_ref, db_ref, dbt_ref, al_ref, alt_ref, *refs,
                   nc, reverse, has_h0):
    if has_h0:
        h0_ref, y_ref, hf_ref, h_ref = refs
    else:
        y_ref, hf_ref, h_ref = refs
    c_idx = pl.program_id(2)
    q = x_ref.shape[0]
    hpg = dt_ref.shape[1]
    p = x_ref.shape[1] // hpg

    @pl.when(c_idx == 0)
    def _():
        h_ref[...] = h0_ref[...] if has_h0 else jnp.zeros_like(h_ref)

    hi = lax.Precision.HIGHEST
    ii = lax.broadcasted_iota(jnp.int32, (q, q), 0)
    jj = lax.broadcasted_iota(jnp.int32, (q, q), 1)
    incl = (jj >= ii) if reverse else (jj <= ii)
    inclf = incl.astype(F32)
    dt = _softplus(dt_ref[...] + db_ref[...])
    dtt = _softplus(dtt_ref[...] + dbt_ref[...])
    da = dt * (-jnp.exp(al_ref[...]))
    dat = dtt * (-jnp.exp(alt_ref[...]))
    acum = jnp.dot(inclf, da, precision=hi, preferred_element_type=F32)
    acumt = lax.dot_general(dat, inclf, (((1,), (1,)), ((), ())), precision=hi,
                            preferred_element_type=F32)
    total = jnp.sum(da, axis=0, keepdims=True)
    xb = x_ref[...]
    bm = b_ref[...].astype(BF16)
    cm = c_ref[...].astype(BF16)
    cb = lax.dot_general(cm, bm, (((1,), (1,)), ((), ())), preferred_element_type=F32)
    h = h_ref[...]
    ch = lax.dot_general(cm, h.astype(BF16), (((1,), (1,)), ((), ())), preferred_element_type=F32)
    w = jnp.exp(total - acum) * dt
    xw, decay = [], []
    for r in range(hpg):
        sl = slice(r * p, (r + 1) * p)
        ai = acum[:, r:r + 1]
        lmat = jnp.exp(jnp.where(incl, ai - acumt[r:r + 1, :], -jnp.inf))
        mat = (cb * lmat * dtt[r:r + 1, :]).astype(BF16)
        xr = xb[:, sl]
        y_ref[:, sl] = jnp.dot(mat, xr.astype(BF16), preferred_element_type=F32) + ch[:, sl] * jnp.exp(ai)
        xw.append(xr * w[:, r:r + 1])
        decay.append(jnp.broadcast_to(jnp.exp(total[:, r:r + 1]), (p, 1)))
    xwt = jnp.concatenate(xw, axis=1).T.astype(BF16)
    h_new = h * jnp.concatenate(decay, axis=0) + jnp.dot(xwt, bm, preferred_element_type=F32)
    h_ref[...] = h_new

    @pl.when(c_idx == nc - 1)
    def _():
        hf_ref[...] = h_new


def _ssd_scan(xbc, dt_r, dt_t, dt_bias, a_log, h0, direction, nb, seq, inner):
    nc = seq // SSD_CHUNK
    hpg = inner // SSD_HEADDIM // SSD_GROUPS
    gw = hpg * SSD_HEADDIM
    reverse = direction == 1
    bo = inner // SSD_STATE
    co = bo + SSD_GROUPS

    def tok(b, c):
        return b * nc + (nc - 1 - c if reverse else c)

    db = dt_bias.reshape(2, SSD_GROUPS, 1, hpg)
    al = a_log.reshape(2, SSD_GROUPS, 1, hpg)
    in_specs = [
        pl.BlockSpec((SSD_CHUNK, gw), lambda b, g, c: (tok(b, c), g)),
        pl.BlockSpec((SSD_CHUNK, SSD_STATE), lambda b, g, c: (tok(b, c), bo + g)),
        pl.BlockSpec((SSD_CHUNK, SSD_STATE), lambda b, g, c: (tok(b, c), co + g)),
        pl.BlockSpec((None, None, SSD_CHUNK, hpg), lambda b, g, c: (direction, g, tok(b, c), 0)),
        pl.BlockSpec((None, None, hpg, SSD_CHUNK), lambda b, g, c: (direction, g, 0, tok(b, c))),
        pl.BlockSpec((None, None, 1, hpg), lambda b, g, c: (direction, g, 0, 0)),
        pl.BlockSpec((None, None, hpg, 1), lambda b, g, c: (direction, g, 0, 0)),
        pl.BlockSpec((None, None, 1, hpg), lambda b, g, c: (direction, g, 0, 0)),
        pl.BlockSpec((None, None, hpg, 1), lambda b, g, c: (direction, g, 0, 0)),
    ]
    args = [xbc, xbc, xbc, dt_r, dt_t, db, db.reshape(2, SSD_GROUPS, hpg, 1), al, al.reshape(2, SSD_GROUPS, hpg, 1)]
    if h0 is not None:
        in_specs.append(pl.BlockSpec((None, None, gw, SSD_STATE), lambda b, g, c: (b, g, 0, 0)))
        args.append(h0)
    body = functools.partial(_ssd_scan_body, nc=nc, reverse=reverse, has_h0=h0 is not None)
    return pl.pallas_call(
        body,
        grid=(nb, SSD_GROUPS, nc),
        in_specs=in_specs,
        out_specs=[
            pl.BlockSpec((SSD_CHUNK, gw), lambda b, g, c: (tok(b, c), g)),
            pl.BlockSpec((None, None, gw, SSD_STATE), lambda b, g, c: (b, g, 0, 0)),
        ],
        out_shape=[
            jax.ShapeDtypeStruct((nb * seq, inner), F32),
            jax.ShapeDtypeStruct((nb, SSD_GROUPS, gw, SSD_STATE), F32),
        ],
        scratch_shapes=[pltpu.VMEM((gw, SSD_STATE), F32)],
        compiler_params=_params("parallel", "parallel", "arbitrary"),
    )(*args)


def _gated_norm_body(yf_ref, yb_ref, xs_ref, z_ref, d_ref, g_ref, o_ref):
    y = yf_ref[...] + yb_ref[...] + d_ref[...] * xs_ref[...]
    z = z_ref[...]
    u = y * (z * jax.nn.sigmoid(z))
    u = u * lax.rsqrt(jnp.mean(u * u, axis=-1, keepdims=True) + EPS)
    o_ref[...] = (u * g_ref[...]).astype(o_ref.dtype)


def _gated_norm(yf, yb, xbc, zx, z_row_off, d_cols, g, inner, tr=256):
    t = yf.shape[0]
    gw = inner // SSD_GROUPS
    tr = _pick(t, tr, 8)
    zo = z_row_off // tr
    spec = pl.BlockSpec((tr, gw), lambda i, j: (i, j))
    vec = pl.BlockSpec((1, gw), lambda i, j: (0, j))
    return pl.pallas_call(
        _gated_norm_body,
        grid=(t // tr, SSD_GROUPS),
        in_specs=[spec, spec, spec, pl.BlockSpec((tr, gw), lambda i, j: (zo + i, j)), vec, vec],
        out_specs=spec,
        out_shape=jax.ShapeDtypeStruct((t, inner), BF16),
        compiler_params=_params("parallel", "parallel"),
    )(yf, yb, xbc, zx, d_cols.reshape(1, inner), g.reshape(1, inner))


def _moe_route(ti, n_experts, bm):
    t = ti.shape[0]
    n_slots = t * TOP_K
    flat_e = ti.reshape(-1)
    onehot = (flat_e[:, None] == jnp.arange(n_experts, dtype=jnp.int32)[None, :]).astype(jnp.int32)
    csum = jnp.cumsum(onehot, axis=0)
    counts = csum[-1]
    rank = jnp.take_along_axis(csum, flat_e[:, None], axis=1)[:, 0] - 1
    padded = (counts + bm - 1) // bm * bm
    pad_end = jnp.cumsum(padded)
    dest = (pad_end - padded)[flat_e] + rank
    n_blocks = n_slots // bm + n_experts
    slot_tok = jnp.zeros((n_blocks * bm,), jnp.int32).at[dest].set(jnp.arange(n_slots, dtype=jnp.int32) // TOP_K)
    starts = jnp.arange(n_blocks, dtype=jnp.int32) * bm
    n_used = pad_end[-1] // bm
    block_exp = jnp.minimum(jnp.sum(starts[:, None] >= pad_end[None, :], axis=1), n_experts - 1).astype(jnp.int32)
    block_exp = jnp.where(jnp.arange(n_blocks) < n_used, block_exp, block_exp[jnp.maximum(n_used - 1, 0)])
    return slot_tok, block_exp, n_used.astype(jnp.int32).reshape(1), dest.reshape(t, TOP_K)


def _row_copy(src_hbm, row, dst, sem):
    return pltpu.make_async_copy(src_hbm.at[pl.ds(row, 1)], dst, sem)


def _gather_rows_body(tok_ref, h_hbm, o_ref, buf, sem):
    bm = buf.shape[0]

    def start(r, carry):
        _row_copy(h_hbm, tok_ref[0, 0, r], buf.at[pl.ds(r, 1)], sem).start()
        return carry

    def wait(r, carry):
        _row_copy(h_hbm, 0, buf.at[pl.ds(r, 1)], sem).wait()
        return carry

    lax.fori_loop(0, bm, start, 0)
    lax.fori_loop(0, bm, wait, 0)
    o_ref[...] = buf[...].astype(o_ref.dtype)


def _gather_rows(h, slot_tok, bm):
    d = h.shape[1]
    nblk = slot_tok.shape[0] // bm
    return pl.pallas_call(
        _gather_rows_body,
        grid=(nblk,),
        in_specs=[
            pl.BlockSpec((1, 1, bm), lambda i: (i, 0, 0), memory_space=pltpu.SMEM),
            pl.BlockSpec(memory_space=pl.ANY),
        ],
        out_specs=pl.BlockSpec((bm, d), lambda i: (i, 0)),
        out_shape=jax.ShapeDtypeStruct((nblk * bm, d), BF16),
        scratch_shapes=[pltpu.VMEM((bm, d), F32), pltpu.SemaphoreType.DMA(())],
        compiler_params=_params("arbitrary"),
    )(slot_tok.reshape(nblk, 1, bm), h)


def _moe_up_body(be_ref, nu_ref, x_ref, wg_ref, wu_ref, bg_ref, bu_ref, o_ref):
    @pl.when(pl.program_id(1) < nu_ref[0])
    def _():
        x = x_ref[...]
        g = jnp.dot(x, wg_ref[...].astype(BF16), preferred_element_type=F32) + bg_ref[...]
        u = jnp.dot(x, wu_ref[...].astype(BF16), preferred_element_type=F32) + bu_ref[...]
        g = jnp.minimum(g, SWIGLU_LIMIT)
        u = jnp.clip(u, -SWIGLU_LIMIT, SWIGLU_LIMIT)
        o_ref[...] = ((u + 1.0) * (g * jax.nn.sigmoid(SWIGLU_ALPHA * g))).astype(o_ref.dtype)

    @pl.when(pl.program_id(1) >= nu_ref[0])
    def _():
        o_ref[...] = jnp.zeros_like(o_ref)


def _moe_up(xs, w, b, layer, block_exp, n_used, bm, tn=256):
    cap, d = xs.shape
    e, f2 = b.shape[1], b.shape[2]
    f = f2 // 2
    tn = _pick(f, tn)
    nf = f // tn
    b4 = b.reshape(b.shape[0], e, 1, f2)
    wspec = lambda off: pl.BlockSpec((None, None, d, tn), lambda j, m, be, nu: (layer, be[m], 0, off + j))
    bspec = lambda off: pl.BlockSpec((None, None, 1, tn), lambda j, m, be, nu: (layer, be[m], 0, off + j))
    return pl.pallas_call(
        _moe_up_body,
        grid_spec=pltpu.PrefetchScalarGridSpec(
            num_scalar_prefetch=2,
            grid=(nf, cap // bm),
            in_specs=[pl.BlockSpec((bm, d), lambda j, m, be, nu: (m, 0)), wspec(0), wspec(nf), bspec(0), bspec(nf)],
            out_specs=pl.BlockSpec((bm, tn), lambda j, m, be, nu: (m, j)),
        ),
        out_shape=jax.ShapeDtypeStruct((cap, f), BF16),
        compiler_params=_params("parallel", "arbitrary"),
    )(block_exp, n_used, xs, w, w, b4, b4)


def _moe_down_body(be_ref, nu_ref, a_ref, w_ref, b_ref, o_ref):
    @pl.when(pl.program_id(1) < nu_ref[0])
    def _():
        o_ref[...] = jnp.dot(a_ref[...], w_ref[...].astype(BF16), preferred_element_type=F32) + b_ref[...]

    @pl.when(pl.program_id(1) >= nu_ref[0])
    def _():
        o_ref[...] = jnp.zeros_like(o_ref)


def _moe_down(act, w, b, layer, block_exp, n_used, bm, tn=1024):
    cap, f = act.shape
    e, d = b.shape[1], b.shape[2]
    tn = _pick(d, tn)
    b4 = b.reshape(b.shape[0], e, 1, d)
    return pl.pallas_call(
        _moe_down_body,
        grid_spec=pltpu.PrefetchScalarGridSpec(
            num_scalar_prefetch=2,
            grid=(d // tn, cap // bm),
            in_specs=[
                pl.BlockSpec((bm, f), lambda j, m, be, nu: (m, 0)),
                pl.BlockSpec((None, None, f, tn), lambda j, m, be, nu: (layer, be[m], 0, j)),
                pl.BlockSpec((None, None, 1, tn), lambda j, m, be, nu: (layer, be[m], 0, j)),
            ],
            out_specs=pl.BlockSpec((bm, tn), lambda j, m, be, nu: (m, j)),
        ),
        out_shape=jax.ShapeDtypeStruct((cap, d), F32),
        compiler_params=_params("parallel", "arbitrary"),
    )(block_exp, n_used, act, w, b4)


def _moe_combine_body(pos_ref, x_ref, tg_ref, g2_ref, y_hbm, o_ref, buf, sem):
    tr = x_ref.shape[0]

    def start(r, carry):
        for k in range(TOP_K):
            _row_copy(y_hbm, pos_ref[0, 0, r * TOP_K + k], buf.at[k, pl.ds(r, 1)], sem).start()
        return carry

    def wait(r, carry):
        for k in range(TOP_K):
            _row_copy(y_hbm, 0, buf.at[k, pl.ds(r, 1)], sem).wait()
        return carry

    lax.fori_loop(0, tr, start, 0)
    lax.fori_loop(0, tr, wait, 0)
    tg = tg_ref[...]
    y = buf[0] * tg[:, 0:1]
    for k in range(1, TOP_K):
        y = y + buf[k] * tg[:, k:k + 1]
    o_ref[...] = x_ref[...] + g2_ref[0] * y


def _moe_combine(x, y_slots, pos, tg, modrows, gate_col, tr=128):
    t, d = x.shape
    rows = t // modrows.shape[0]
    tr = _pick(rows, tr, 8)
    per = rows // tr
    return pl.pallas_call(
        _moe_combine_body,
        grid=(t // tr,),
        in_specs=[
            pl.BlockSpec((1, 1, tr * TOP_K), lambda i: (i, 0, 0), memory_space=pltpu.SMEM),
            pl.BlockSpec((tr, d), lambda i: (i, 0)),
            pl.BlockSpec((tr, V7X_LANES), lambda i: (i, 0)),
            pl.BlockSpec((1, 1, d), lambda i: (i // per, 0, gate_col)),
            pl.BlockSpec(memory_space=pl.ANY),
        ],
        out_specs=pl.BlockSpec((tr, d), lambda i: (i, 0)),
        out_shape=jax.ShapeDtypeStruct((t, d), F32),
        scratch_shapes=[pltpu.VMEM((TOP_K, tr, d), F32), pltpu.SemaphoreType.DMA(())],
        compiler_params=_params("arbitrary"),
    )(pos.reshape(t // tr, 1, tr * TOP_K), x, tg, modrows, y_slots)


def _moe_ffn(x, g, modrows, layer, router_w, router_b, w_gate_up, b_gate_up, w_down, b_down):
    n_experts = router_w.shape[-1]
    bm = MOE_BLOCK_ROWS
    h, ti, tg = _modulate_router(x, g, modrows, 3, router_w[layer], router_b[layer])
    slot_tok, block_exp, n_used, pos = _moe_route(ti[:, :TOP_K], n_experts, bm)
    xs = _gather_rows(h, slot_tok, bm)
    act = _moe_up(xs, w_gate_up, b_gate_up, layer, block_exp, n_used, bm)
    y_slots = _moe_down(act, w_down, b_down, layer, block_exp, n_used, bm)
    return _moe_combine(x, y_slots, pos, tg, modrows, 5)


def kernel(x_prompt, x_sample, cache_attn_k, cache_attn_v, state_ssd_fwd, state_ssd_bwd, c, c_ctx, w_ada, b_ada, norm1_g, norm2_g, fnet_w_out, fnet_b_out, attn_w_qkv, attn_q_gain, attn_k_gain, attn_sink, attn_w_out, ssd_w_in, ssd_conv_w, ssd_conv_b, ssd_dt_bias, ssd_a_log, ssd_d, ssd_norm_g, ssd_w_out, moe_router_w, moe_router_b, moe_w_gate_up, moe_b_gate_up, moe_w_down, moe_b_down):
    bp, lp, d = x_prompt.shape
    bs, ls, _ = x_sample.shape
    tp, ts = bp * lp, bs * ls
    t = tp + ts
    depth = w_ada.shape[0]
    rows = math.gcd(lp, ls)
    x = jnp.concatenate([x_prompt.reshape(tp, d), x_sample.reshape(ts, d)], axis=0)

    assert 1 + bs <= COND_ROWS
    cond = jnp.zeros((COND_ROWS, d), F32).at[0].set(c_ctx).at[1:1 + bs].set(c)
    mod = _matmul_wbatched(cond, w_ada, b_ada, silu_in=True, tm=COND_ROWS, tn=1024, tk=2048)
    group_cond = jnp.concatenate([jnp.zeros((tp // rows,), jnp.int32),
                                  1 + jnp.arange(ts // rows, dtype=jnp.int32) // (ls // rows)])

    q_dim, kv_dim = N_HEADS * HEAD_DIM, N_KV * HEAD_DIM
    new_k, new_v, new_f, new_b = [], [], [], []
    for l in range(depth):
        kind, j = l % N_MIXERS, l // N_MIXERS
        modrows = mod[l][group_cond].reshape(t // rows, 1, 6 * d)
        h = _modulate(x, norm1_g[l], modrows, 0)
        if kind == 0:
            mixed = jnp.concatenate([_fourier_mix(h, 0, bp, lp), _fourier_mix(h, tp, bs, ls)], axis=0)
            x = _matmul(mixed, fnet_w_out, w_lead=(j,), bias=fnet_b_out[j], res=x, gate=modrows, gate_col=2)
        elif kind == 1:
            qkv = _matmul(h, attn_w_qkv, w_lead=(j,))
            cos_t, sin_t = _rope_tables(ls, rows)
            n_tab = ls // rows
            tab = lambda i: jnp.where(i < tp // rows, n_tab, (i - tp // rows) % n_tab)
            qn = _qknorm_rope(qkv, attn_q_gain[j], cos_t, sin_t, 0, q_dim, rows, tab, BF16)
            kn = _qknorm_rope(qkv, attn_k_gain[j], cos_t, sin_t, q_dim, kv_dim, rows, tab, F32)
            new_k.append(kn[:tp].reshape(bp, lp, N_KV, HEAD_DIM))
            new_v.append(qkv[:tp, q_dim + kv_dim:].reshape(bp, lp, N_KV, HEAD_DIM))
            op = _attention(qn, kn, qkv, q_dim + kv_dim, attn_sink[j], None, None, 0, bp, lp, False)
            kc = cache_attn_k[:, j].reshape(bs, -1, kv_dim)
            vc = cache_attn_v[:, j].reshape(bs, -1, kv_dim)
            os_ = _attention(qn, kn, qkv, q_dim + kv_dim, attn_sink[j], kc, vc, tp, bs, ls, True)
            o = jnp.concatenate([op, os_], axis=0)
            x = _matmul(o, attn_w_out, w_lead=(j,), res=x, gate=modrows, gate_col=2)
        else:
            inner = ssd_w_out.shape[1]
            heads = inner // SSD_HEADDIM
            hpg = heads // SSD_GROUPS
            gn = SSD_GROUPS * SSD_STATE
            n_zx = 2 * inner + 2 * gn
            tn_zx = _pick(n_zx, 1024)
            zx = _matmul(h, ssd_w_in, w_lead=(j,), n_out=n_zx, tn=tn_zx)
            dt_raw = _matmul(h, ssd_w_in, w_lead=(j,), n_off=n_zx // (2 * heads), n_out=2 * heads, tn=2 * heads)
            dt_r = dt_raw.reshape(t, 2, SSD_GROUPS, hpg).transpose(1, 2, 0, 3)
            dt_t = dt_raw.reshape(t, 2, SSD_GROUPS, hpg).transpose(1, 2, 3, 0)
            d_cols = jnp.repeat(ssd_d[j], SSD_HEADDIM)
            ys = []
            for (row_off, nb, seq, s_f, s_b) in ((0, bp, lp, None, None),
                                                 (tp, bs, ls, state_ssd_fwd[:, j], state_ssd_bwd[:, j])):
                xbc = _conv_silu(zx, inner, inner + 2 * gn, ssd_conv_w[j], ssd_conv_b[j], row_off, nb, seq)
                sl = slice(row_off, row_off + nb * seq)
                h0 = [None if s is None else s.reshape(nb, SSD_GROUPS, hpg * SSD_HEADDIM, SSD_STATE) for s in (s_f, s_b)]
                yf, hf = _ssd_scan(xbc, dt_r[:, :, sl], dt_t[:, :, :, sl], ssd_dt_bias[j], ssd_a_log[j], h0[0], 0, nb, seq, inner)
                yb, hb = _ssd_scan(xbc, dt_r[:, :, sl], dt_t[:, :, :, sl], ssd_dt_bias[j], ssd_a_log[j], h0[1], 1, nb, seq, inner)
                if s_f is None:
                    new_f.append(hf.reshape(nb, heads, SSD_HEADDIM, SSD_STATE))
                    new_b.append(hb.reshape(nb, heads, SSD_HEADDIM, SSD_STATE))
                ys.append(_gated_norm(yf, yb, xbc, zx, row_off, d_cols, ssd_norm_g[j], inner))
            x = _matmul(jnp.concatenate(ys, axis=0), ssd_w_out, w_lead=(j,), res=x, gate=modrows, gate_col=2)
        x = _moe_ffn(x, norm2_g[l], modrows, l, moe_router_w, moe_router_b,
                     moe_w_gate_up, moe_b_gate_up, moe_w_down, moe_b_down)

    return (x[:tp].reshape(bp, lp, d), x[tp:].reshape(bs, ls, d),
            jnp.stack(new_k, axis=1), jnp.stack(new_v, axis=1),
            jnp.stack(new_f, axis=1), jnp.stack(new_b, axis=1))
```

```python
import functools
import math

import jax
import jax.numpy as jnp
from jax import lax
from jax.experimental import pallas as pl
from jax.experimental.pallas import tpu as pltpu

GRID_W = 64
N_MIXERS = 3
EPS = 1e-6
FNET_GROUPS = 8
N_HEADS = 32
N_KV = 8
HEAD_DIM = 128
WINDOW = 128
BLOCK = 128
ROPE_THETA = 10000.0
NEG_INF = -1e30
SSD_HEADDIM = 64
SSD_GROUPS = 8
SSD_STATE = 128
SSD_CONV = 5
SSD_CHUNK = 128
TOP_K = 4
SWIGLU_LIMIT = 7.0
SWIGLU_ALPHA = 1.702

V7X_LANES = 128
V7X_VMEM_LIMIT_BYTES = 56 * 1024 * 1024
COND_ROWS = 16
MOE_BLOCK_ROWS = 512
MOE_DOWN_COLS = 2048
DMA_UNROLL = 8

F32 = jnp.float32
BF16 = jnp.bfloat16


def _params(*sem):
    return pltpu.CompilerParams(dimension_semantics=sem, vmem_limit_bytes=V7X_VMEM_LIMIT_BYTES)


def _pick(dim, pref, align=V7X_LANES):
    if dim <= pref:
        return dim
    t = pref - pref % align
    while t >= align:
        if dim % t == 0:
            return t
        t -= align
    return dim


def _mm_body(*refs, nk, k_axis, silu_in, has_bias, has_res, groups):
    it = iter(refs)
    x_ref, w_ref = next(it), next(it)
    b_ref = next(it) if has_bias else None
    r_ref = next(it) if has_res else None
    g_ref = next(it) if has_res else None
    o_ref, acc_ref = next(it), next(it)
    k = pl.program_id(k_axis)

    @pl.when(k == 0)
    def _():
        acc_ref[...] = jnp.zeros_like(acc_ref)

    x = x_ref[...]
    if silu_in:
        x = x.astype(F32)
        x = x * jax.nn.sigmoid(x)
    acc_ref[...] += jnp.dot(x.astype(BF16), w_ref[...].astype(BF16), preferred_element_type=F32)

    @pl.when(k == nk - 1)
    def _():
        r = acc_ref[...]
        if has_bias:
            r = r + b_ref[...]
        if has_res:
            tm, tn = r.shape
            r = (r.reshape(groups, tm // groups, tn) * g_ref[...]).reshape(tm, tn) + r_ref[...]
        o_ref[...] = r.astype(o_ref.dtype)


def _matmul(x, w, *, w_lead=(), n_off=0, n_out=None, bias=None, res=None, gate=None, gate_col=0,
            out_dtype=F32, tm=1024, tn=1024, tk=1024, name="matmul"):
    m, kdim = x.shape
    n = w.shape[-1] if n_out is None else n_out
    tm, tn, tk = _pick(m, tm, 8), _pick(n, tn), _pick(kdim, tk)
    nk = kdim // tk
    lead = tuple(w_lead)
    in_specs = [
        pl.BlockSpec((tm, tk), lambda i, j, k: (i, k)),
        pl.BlockSpec((None,) * len(lead) + (tk, tn), lambda i, j, k: lead + (k, j + n_off)),
    ]
    args = [x, w]
    if bias is not None:
        in_specs.append(pl.BlockSpec((1, tn), lambda i, j, k: (0, j)))
        args.append(bias.reshape(1, n))
    groups = 1
    if res is not None:
        rows = m // gate.shape[0]
        assert tm % rows == 0
        groups = tm // rows
        ncol = n // tn
        in_specs.append(pl.BlockSpec((tm, tn), lambda i, j, k: (i, j)))
        in_specs.append(pl.BlockSpec((groups, 1, tn), lambda i, j, k: (i, 0, gate_col * ncol + j)))
        args += [res, gate]
    body = functools.partial(_mm_body, nk=nk, k_axis=2, silu_in=False, has_bias=bias is not None,
                             has_res=res is not None, groups=groups)
    return pl.pallas_call(
        body,
        grid=(m // tm, n // tn, nk),
        in_specs=in_specs,
        out_specs=pl.BlockSpec((tm, tn), lambda i, j, k: (i, j)),
        out_shape=jax.ShapeDtypeStruct((m, n), out_dtype),
        scratch_shapes=[pltpu.VMEM((tm, tn), F32)],
        compiler_params=_params("parallel", "parallel", "arbitrary"),
        name=name,
    )(*args)


def _matmul_wbatched(x, w, bias=None, *, silu_in=False, out_dtype=F32, tm=1024, tn=1024, tk=512):
    m, kdim = x.shape
    nb, _, n = w.shape
    tm, tn, tk = _pick(m, tm, 8), _pick(n, tn), _pick(kdim, tk)
    nk = kdim // tk
    in_specs = [
        pl.BlockSpec((tm, tk), lambda b, i, j, k: (i, k)),
        pl.BlockSpec((None, tk, tn), lambda b, i, j, k: (b, k, j)),
    ]
    args = [x, w]
    if bias is not None:
        in_specs.append(pl.BlockSpec((None, 1, tn), lambda b, i, j, k: (b, 0, j)))
        args.append(bias.reshape(nb, 1, n))
    body = functools.partial(_mm_body, nk=nk, k_axis=3, silu_in=silu_in, has_bias=bias is not None,
                             has_res=False, groups=1)
    return pl.pallas_call(
        body,
        grid=(nb, m // tm, n // tn, nk),
        in_specs=in_specs,
        out_specs=pl.BlockSpec((None, tm, tn), lambda b, i, j, k: (b, i, j)),
        out_shape=jax.ShapeDtypeStruct((nb, m, n), out_dtype),
        scratch_shapes=[pltpu.VMEM((tm, tn), F32)],
        compiler_params=_params("parallel", "parallel", "parallel", "arbitrary"),
        name="adaln",
    )(*args)


def _modulated(x, g_ref, sh_ref, sc_ref):
    y = x * lax.rsqrt(jnp.mean(x * x, axis=-1, keepdims=True) + EPS) * g_ref[...]
    return y * (1.0 + sc_ref[0]) + sh_ref[0]


def _modulate_body(x_ref, g_ref, sh_ref, sc_ref, o_ref):
    o_ref[...] = _modulated(x_ref[...], g_ref, sh_ref, sc_ref).astype(o_ref.dtype)


def _mod_specs(rows, d, col):
    return [
        pl.BlockSpec((rows, d), lambda i: (i, 0)),
        pl.BlockSpec((1, d), lambda i: (0, 0)),
        pl.BlockSpec((1, 1, d), lambda i: (i, 0, col)),
        pl.BlockSpec((1, 1, d), lambda i: (i, 0, col + 1)),
    ]


def _modulate(x, g, modrows, col):
    t, d = x.shape
    rows = t // modrows.shape[0]
    return pl.pallas_call(
        _modulate_body,
        grid=(t // rows,),
        in_specs=_mod_specs(rows, d, col),
        out_specs=pl.BlockSpec((rows, d), lambda i: (i, 0)),
        out_shape=jax.ShapeDtypeStruct((t, d), BF16),
        compiler_params=_params("parallel"),
        name="modulate",
    )(x, g.reshape(1, d), modrows, modrows)


def _pack_bf16_pairs(x):
    n = x.shape[1] // 2
    lo = pltpu.bitcast(x[:, :n].astype(BF16).astype(F32), jnp.uint32)
    hi = pltpu.bitcast(x[:, n:].astype(BF16).astype(F32), jnp.uint32)
    return (hi & jnp.uint32(0xFFFF0000)) | (lo >> 16)


def _unpack_bf16_pairs(u):
    return pltpu.bitcast(u << 16, F32), pltpu.bitcast(u & jnp.uint32(0xFFFF0000), F32)


def _modulate_router_body(x_ref, g_ref, sh_ref, sc_ref, rw_ref, rb_ref, h_ref, ti_ref, tg_ref):
    h = _modulated(x_ref[...], g_ref, sh_ref, sc_ref)
    h_ref[...] = _pack_bf16_pairs(h)
    logits = jnp.dot(h, rw_ref[...], precision=lax.Precision.HIGHEST, preferred_element_type=F32) + rb_ref[...]
    lane = lax.broadcasted_iota(jnp.int32, logits.shape, 1).astype(F32)
    n_lane = float(logits.shape[1])
    vals, idxs = [], []
    for _ in range(TOP_K):
        m = jnp.max(logits, axis=-1, keepdims=True)
        idx = jnp.min(jnp.where(logits == m, lane, n_lane), axis=-1, keepdims=True)
        vals.append(m)
        idxs.append(idx)
        logits = jnp.where(lane == idx, -jnp.inf, logits)
    es = [jnp.exp(v - vals[0]) for v in vals]
    inv = 1.0 / functools.reduce(lambda a, b: a + b, es)
    ti = jnp.zeros(lane.shape, jnp.int32)
    tg = jnp.zeros(lane.shape, F32)
    for k in range(TOP_K):
        ti = jnp.where(lane == k, idxs[k].astype(jnp.int32), ti)
        tg = jnp.where(lane == k, es[k] * inv, tg)
    ti_ref[...] = ti
    tg_ref[...] = tg


def _modulate_router(x, g, modrows, col, rw, rb):
    t, d = x.shape
    rows = t // modrows.shape[0]
    e = rw.shape[1]
    rw_p = jnp.zeros((d, V7X_LANES), F32).at[:, :e].set(rw)
    rb_p = jnp.full((1, V7X_LANES), NEG_INF, F32).at[0, :e].set(rb)
    return pl.pallas_call(
        _modulate_router_body,
        grid=(t // rows,),
        in_specs=_mod_specs(rows, d, col) + [
            pl.BlockSpec((d, V7X_LANES), lambda i: (0, 0)),
            pl.BlockSpec((1, V7X_LANES), lambda i: (0, 0)),
        ],
        out_specs=[
            pl.BlockSpec((rows, d // 2), lambda i: (i, 0)),
            pl.BlockSpec((rows, V7X_LANES), lambda i: (i, 0)),
            pl.BlockSpec((rows, V7X_LANES), lambda i: (i, 0)),
        ],
        out_shape=[
            jax.ShapeDtypeStruct((t, d // 2), jnp.uint32),
            jax.ShapeDtypeStruct((t, V7X_LANES), jnp.int32),
            jax.ShapeDtypeStruct((t, V7X_LANES), F32),
        ],
        compiler_params=_params("parallel"),
        name="modulate_router",
    )(x, g.reshape(1, d), modrows, modrows, rw_p, rb_p)


def _dft_tables(n):
    j = jnp.arange(n, dtype=jnp.int32)
    ang = ((j[:, None] * j[None, :]) % n).astype(F32) * (2.0 * math.pi / n)
    s = 1.0 / math.sqrt(n)
    return jnp.cos(ang) * s, jnp.sin(ang) * s


def _fnet_channel_body(x_ref, w_ref, a_ref, b_ref):
    c = a_ref.shape[1]
    r = jnp.dot(x_ref[...], w_ref[...], preferred_element_type=F32)
    a_ref[...] = r[:, :c].astype(a_ref.dtype)
    b_ref[...] = r[:, c:].astype(b_ref.dtype)


def _fnet_channel_dft(h, tm=1024):
    t, d = h.shape
    c = d // FNET_GROUPS
    cos_c, sin_c = _dft_tables(c)
    cs = jnp.concatenate([cos_c, sin_c], axis=1).astype(BF16)
    tm = _pick(t, tm, 8)
    spec = pl.BlockSpec((tm, c), lambda i, g: (i, g))
    return pl.pallas_call(
        _fnet_channel_body,
        grid=(t // tm, FNET_GROUPS),
        in_specs=[spec, pl.BlockSpec((c, 2 * c), lambda i, g: (0, 0))],
        out_specs=[spec, spec],
        out_shape=[jax.ShapeDtypeStruct((t, d), BF16)] * 2,
        compiler_params=_params("parallel", "parallel"),
        name="fnet_channel_dft",
    )(h, cs)


def _fnet_seq_body(c_ref, s_ref, a_ref, b_ref, o_ref, acc_ref, *, nk):
    k = pl.program_id(3)

    @pl.when(k == 0)
    def _():
        acc_ref[...] = jnp.zeros_like(acc_ref)

    acc_ref[...] += (jnp.dot(c_ref[...], a_ref[...], preferred_element_type=F32)
                     + jnp.dot(s_ref[...], b_ref[...], preferred_element_type=F32))

    @pl.when(k == nk - 1)
    def _():
        o_ref[...] = acc_ref[...].astype(o_ref.dtype)


def _fnet_seq_dft(a, b, row_off, nb, seq, tm=1024, tn=1024, tk=1024):
    d = a.shape[1]
    cos_l, sin_l = _dft_tables(seq)
    cos_l, nsin_l = cos_l.astype(BF16), (-sin_l).astype(BF16)
    tm, tn, tk = _pick(seq, tm, 8), _pick(d, tn), _pick(seq, tk)
    nk, mt = seq // tk, seq // tm
    ro = row_off // tk
    tab = pl.BlockSpec((tm, tk), lambda bb, i, j, k: (i, k))
    src = pl.BlockSpec((tk, tn), lambda bb, i, j, k: (ro + bb * nk + k, j))
    return pl.pallas_call(
        functools.partial(_fnet_seq_body, nk=nk),
        grid=(nb, mt, d // tn, nk),
        in_specs=[tab, tab, src, src],
        out_specs=pl.BlockSpec((tm, tn), lambda bb, i, j, k: (bb * mt + i, j)),
        out_shape=jax.ShapeDtypeStruct((nb * seq, d), BF16),
        scratch_shapes=[pltpu.VMEM((tm, tn), F32)],
        compiler_params=_params("parallel", "parallel", "parallel", "arbitrary"),
        name="fnet_seq_dft",
    )(cos_l, nsin_l, a, b)


def _rope_tables(seq, rows_identity):
    half = HEAD_DIM // 2
    inv_freq = ROPE_THETA ** (-jnp.arange(0, half, 2, dtype=F32) / half)
    rows = seq // GRID_W
    row_pos = jnp.repeat(jnp.arange(rows, dtype=jnp.int32), GRID_W).astype(F32)
    col_pos = jnp.tile(jnp.arange(GRID_W, dtype=jnp.int32), rows).astype(F32)
    ar = row_pos[:, None] * inv_freq
    ac = col_pos[:, None] * inv_freq
    cos_t = jnp.concatenate([jnp.cos(ar), jnp.cos(ar), jnp.cos(ac), jnp.cos(ac)], axis=-1)
    sin_t = jnp.concatenate([-jnp.sin(ar), jnp.sin(ar), -jnp.sin(ac), jnp.sin(ac)], axis=-1)
    cos_t = jnp.concatenate([cos_t, jnp.ones((rows_identity, HEAD_DIM), F32)], axis=0)
    sin_t = jnp.concatenate([sin_t, jnp.zeros((rows_identity, HEAD_DIM), F32)], axis=0)
    return cos_t, sin_t


def _qknorm_rope_body(x_ref, g_ref, cos_ref, sin_ref, o_ref):
    cos_t, sin_t, gain = cos_ref[...], sin_ref[...], g_ref[...]
    lane = lax.broadcasted_iota(jnp.int32, cos_t.shape, 1)
    first = (lane % (HEAD_DIM // 2)) < (HEAD_DIM // 4)
    for h in range(x_ref.shape[1] // HEAD_DIM):
        sl = slice(h * HEAD_DIM, (h + 1) * HEAD_DIM)
        x = x_ref[:, sl]
        y = x * lax.rsqrt(jnp.mean(x * x, axis=-1, keepdims=True) + EPS) * gain
        partner = jnp.where(first, pltpu.roll(y, HEAD_DIM - HEAD_DIM // 4, 1), pltpu.roll(y, HEAD_DIM // 4, 1))
        o_ref[:, sl] = (y * cos_t + partner * sin_t).astype(o_ref.dtype)


def _qknorm_rope(qkv, gain, cos_t, sin_t, col_off, n_cols, rows, tab_index, out_dtype, tc=1024):
    t = qkv.shape[0]
    tc = _pick(n_cols, tc)
    off = col_off // tc
    return pl.pallas_call(
        _qknorm_rope_body,
        grid=(t // rows, n_cols // tc),
        in_specs=[
            pl.BlockSpec((rows, tc), lambda i, j: (i, off + j)),
            pl.BlockSpec((1, HEAD_DIM), lambda i, j: (0, 0)),
            pl.BlockSpec((rows, HEAD_DIM), lambda i, j: (tab_index(i), 0)),
            pl.BlockSpec((rows, HEAD_DIM), lambda i, j: (tab_index(i), 0)),
        ],
        out_specs=pl.BlockSpec((rows, tc), lambda i, j: (i, j)),
        out_shape=jax.ShapeDtypeStruct((t, n_cols), out_dtype),
        compiler_params=_params("parallel", "parallel"),
        name="qknorm_rope",
    )(qkv, gain.reshape(1, HEAD_DIM), cos_t, sin_t)


def _attn_body(sink_ref, q_ref, *refs, n_win, seq_len):
    kw, vw = refs[:n_win], refs[n_win:2 * n_win]
    kc_ref, vc_ref, o_ref = refs[2 * n_win:]
    g, i = pl.program_id(1), pl.program_id(2)
    rep = q_ref.shape[1] // HEAD_DIM
    nq = q_ref.shape[0]
    q = jnp.concatenate([q_ref[:, r * HEAD_DIM:(r + 1) * HEAD_DIM] for r in range(rep)], axis=0).astype(BF16)
    keys = jnp.concatenate([r[...].astype(BF16) for r in kw] + [kc_ref[...].astype(BF16)], axis=0)
    vals = jnp.concatenate([r[...].astype(BF16) for r in vw] + [vc_ref[...].astype(BF16)], axis=0)
    s = lax.dot_general(q, keys, (((1,), (1,)), ((), ())), preferred_element_type=F32) * (HEAD_DIM ** -0.5)
    if n_win:
        qpos = i * BLOCK + lax.broadcasted_iota(jnp.int32, s.shape, 0) % nq
        kj = lax.broadcasted_iota(jnp.int32, s.shape, 1)
        kpos = i * BLOCK - BLOCK + kj
        band = (jnp.abs(qpos - kpos) <= WINDOW) & (kpos >= 0) & (kpos < seq_len)
        s = jnp.where(band | (kj >= n_win * BLOCK), s, NEG_INF)
    sink = jnp.concatenate([jnp.full((nq, 1), sink_ref[g * rep + r], F32) for r in range(rep)], axis=0)
    m = jnp.maximum(jnp.max(s, axis=-1, keepdims=True), sink)
    p = jnp.exp(s - m)
    inv = 1.0 / (jnp.sum(p, axis=-1, keepdims=True) + jnp.exp(sink - m))
    o = jnp.dot((p * inv).astype(BF16), vals, preferred_element_type=F32)
    for r in range(rep):
        o_ref[:, r * HEAD_DIM:(r + 1) * HEAD_DIM] = o[r * nq:(r + 1) * nq].astype(o_ref.dtype)


def _attention(qn, kn, qkv, v_col_off, sink, kc, vc, row_off, nb, seq, windowed):
    rep = N_HEADS // N_KV
    qw = rep * HEAD_DIM
    vo = v_col_off // HEAD_DIM
    if windowed:
        nblk = seq // BLOCK
        ro = row_off // BLOCK
        lc = kc.shape[1]

        def win(shift, col):
            return pl.BlockSpec(
                (BLOCK, HEAD_DIM),
                lambda b, g, i, s: (ro + b * nblk + jnp.clip(i + shift, 0, nblk - 1), col + g))

        in_specs = [pl.BlockSpec((BLOCK, qw), lambda b, g, i, s: (ro + b * nblk + i, g))]
        in_specs += [win(-1, 0), win(0, 0), win(1, 0), win(-1, vo), win(0, vo), win(1, vo)]
        in_specs += [pl.BlockSpec((None, lc, HEAD_DIM), lambda b, g, i, s: (b, 0, g))] * 2
        args = [qn, kn, kn, kn, qkv, qkv, qkv, kc, vc]
        grid = (nb, N_KV, nblk)
        out_spec = pl.BlockSpec((BLOCK, qw), lambda b, g, i, s: (b * nblk + i, g))
        body = functools.partial(_attn_body, n_win=3, seq_len=seq)
    else:
        ro = row_off // seq
        in_specs = [
            pl.BlockSpec((seq, qw), lambda b, g, i, s: (ro + b, g)),
            pl.BlockSpec((seq, HEAD_DIM), lambda b, g, i, s: (ro + b, g)),
            pl.BlockSpec((seq, HEAD_DIM), lambda b, g, i, s: (ro + b, vo + g)),
        ]
        args = [qn, kn, qkv]
        grid = (nb, N_KV, 1)
        out_spec = pl.BlockSpec((seq, qw), lambda b, g, i, s: (b, g))
        body = functools.partial(_attn_body, n_win=0, seq_len=seq)
    return pl.pallas_call(
        body,
        grid_spec=pltpu.PrefetchScalarGridSpec(
            num_scalar_prefetch=1, grid=grid, in_specs=in_specs, out_specs=out_spec),
        out_shape=jax.ShapeDtypeStruct((nb * seq, N_HEADS * HEAD_DIM), BF16),
        compiler_params=_params("parallel", "parallel", "parallel"),
        name="attention_window" if windowed else "attention_context",
    )(sink.astype(F32), *args)


def _conv_silu_body(x_ref, w_ref, b_ref, o_ref):
    x = x_ref[...]
    seq = x.shape[0]
    pad = SSD_CONV // 2
    t = lax.broadcasted_iota(jnp.int32, (seq, 1), 0)
    acc = b_ref[...] + x * w_ref[pad:pad + 1, :]
    for k in range(SSD_CONV):
        s = k - pad
        if s == 0:
            continue
        shifted = pltpu.roll(x, (-s) % seq, 0)
        acc = acc + jnp.where((t + s >= 0) & (t + s < seq), shifted, 0.0) * w_ref[k:k + 1, :]
    o_ref[...] = acc * jax.nn.sigmoid(acc)


def _conv_silu(zx, col_off, n_cols, w, b, row_off, nb, seq, tc=256):
    tc = _pick(n_cols, tc)
    co, ro = col_off // tc, row_off // seq
    return pl.pallas_call(
        _conv_silu_body,
        grid=(nb, n_cols // tc),
        in_specs=[
            pl.BlockSpec((seq, tc), lambda b, j: (ro + b, co + j)),
            pl.BlockSpec((SSD_CONV, tc), lambda b, j: (0, j)),
            pl.BlockSpec((1, tc), lambda b, j: (0, j)),
        ],
        out_specs=pl.BlockSpec((seq, tc), lambda b, j: (b, j)),
        out_shape=jax.ShapeDtypeStruct((nb * seq, n_cols), F32),
        compiler_params=_params("parallel", "parallel"),
        name="ssd_conv_silu",
    )(zx, w, b.reshape(1, n_cols))


def _softplus(x):
    return jnp.maximum(x, 0.0) + jnp.log(1.0 + jnp.exp(-jnp.abs(x)))


def _ssd_scan_body(x_ref, b_ref, c_ref, dt_ref, dtt_ref, db_ref, dbt_ref, al_ref, alt_ref, *refs,
                   nc, reverse, has_h0):
    if has_h0:
        h0_ref, y_ref, hf_ref, h_ref = refs
    else:
        y_ref, hf_ref, h_ref = refs
    c_idx = pl.program_id(2)
    q = x_ref.shape[0]
    hpg = dt_ref.shape[1]
    p = x_ref.shape[1] // hpg

    @pl.when(c_idx == 0)
    def _():
        h_ref[...] = h0_ref[...] if has_h0 else jnp.zeros_like(h_ref)

    hi = lax.Precision.HIGHEST
    ii = lax.broadcasted_iota(jnp.int32, (q, q), 0)
    jj = lax.broadcasted_iota(jnp.int32, (q, q), 1)
    incl = (jj >= ii) if reverse else (jj <= ii)
    inclf = incl.astype(F32)
    dt = _softplus(dt_ref[...] + db_ref[...])
    dtt = _softplus(dtt_ref[...] + dbt_ref[...])
    da = dt * (-jnp.exp(al_ref[...]))
    dat = dtt * (-jnp.exp(alt_ref[...]))
    acum = jnp.dot(inclf, da, precision=hi, preferred_element_type=F32)
    acumt = lax.dot_general(dat, inclf, (((1,), (1,)), ((), ())), precision=hi,
                            preferred_element_type=F32)
    total = jnp.sum(da, axis=0, keepdims=True)
    total_t = jnp.sum(dat, axis=1, keepdims=True)
    xb = x_ref[...]
    bm = b_ref[...].astype(BF16)
    cm = c_ref[...].astype(BF16)
    cb = lax.dot_general(cm, bm, (((1,), (1,)), ((), ())), preferred_element_type=F32)
    h = h_ref[...]
    ch = lax.dot_general(cm, h.astype(BF16), (((1,), (1,)), ((), ())), preferred_element_type=F32)
    lane = lax.broadcasted_iota(jnp.int32, (q, 2 * p), 1)
    decay = []
    for r in range(0, hpg, 2):
        sl = slice(r * p, (r + 2) * p)
        mats, ex = [], []
        for rr in (r, r + 1):
            ai = jnp.broadcast_to(acum[:, rr:rr + 1], (q, q))
            lmat = jnp.exp(jnp.where(incl, ai - acumt[rr:rr + 1, :], -jnp.inf))
            mats.append((cb * lmat * dtt[rr:rr + 1, :]).astype(BF16))
            ex.append(jnp.exp(ai[:, :2 * p]))
            decay.append(jnp.broadcast_to(jnp.exp(total[:, rr:rr + 1]), (p, 1)))
        x2 = xb[:, sl]
        rhs = jnp.concatenate([jnp.where(lane < p, x2, 0.0), jnp.where(lane >= p, x2, 0.0)], axis=0).astype(BF16)
        intra = jnp.dot(jnp.concatenate(mats, axis=1), rhs, preferred_element_type=F32)
        y_ref[:, sl] = intra + ch[:, sl] * jnp.where(lane < p, ex[0], ex[1])
    w_t = jnp.exp(total_t - acumt) * dtt
    xwt = (xb.T.reshape(hpg, p, q) * w_t[:, None, :]).reshape(hpg * p, q).astype(BF16)
    h_new = h * jnp.concatenate(decay, axis=0) + jnp.dot(xwt, bm, preferred_element_type=F32)
    h_ref[...] = h_new

    @pl.when(c_idx == nc - 1)
    def _():
        hf_ref[...] = h_new


def _ssd_scan(xbc, dt_r, dt_t, dt_bias, a_log, h0, direction, nb, seq, inner):
    nc = seq // SSD_CHUNK
    hpg = inner // SSD_HEADDIM // SSD_GROUPS
    gw = hpg * SSD_HEADDIM
    reverse = direction == 1
    bo = inner // SSD_STATE
    co = bo + SSD_GROUPS

    def tok(b, c):
        return b * nc + (nc - 1 - c if reverse else c)

    db = dt_bias.reshape(2, SSD_GROUPS, 1, hpg)
    al = a_log.reshape(2, SSD_GROUPS, 1, hpg)
    in_specs = [
        pl.BlockSpec((SSD_CHUNK, gw), lambda b, g, c: (tok(b, c), g)),
        pl.BlockSpec((SSD_CHUNK, SSD_STATE), lambda b, g, c: (tok(b, c), bo + g)),
        pl.BlockSpec((SSD_CHUNK, SSD_STATE), lambda b, g, c: (tok(b, c), co + g)),
        pl.BlockSpec((None, None, SSD_CHUNK, hpg), lambda b, g, c: (direction, g, tok(b, c), 0)),
        pl.BlockSpec((None, None, hpg, SSD_CHUNK), lambda b, g, c: (direction, g, 0, tok(b, c))),
        pl.BlockSpec((None, None, 1, hpg), lambda b, g, c: (direction, g, 0, 0)),
        pl.BlockSpec((None, None, hpg, 1), lambda b, g, c: (direction, g, 0, 0)),
        pl.BlockSpec((None, None, 1, hpg), lambda b, g, c: (direction, g, 0, 0)),
        pl.BlockSpec((None, None, hpg, 1), lambda b, g, c: (direction, g, 0, 0)),
    ]
    args = [xbc, xbc, xbc, dt_r, dt_t, db, db.reshape(2, SSD_GROUPS, hpg, 1), al, al.reshape(2, SSD_GROUPS, hpg, 1)]
    if h0 is not None:
        in_specs.append(pl.BlockSpec((None, None, gw, SSD_STATE), lambda b, g, c: (b, g, 0, 0)))
        args.append(h0)
    body = functools.partial(_ssd_scan_body, nc=nc, reverse=reverse, has_h0=h0 is not None)
    return pl.pallas_call(
        body,
        grid=(nb, SSD_GROUPS, nc),
        in_specs=in_specs,
        out_specs=[
            pl.BlockSpec((SSD_CHUNK, gw), lambda b, g, c: (tok(b, c), g)),
            pl.BlockSpec((None, None, gw, SSD_STATE), lambda b, g, c: (b, g, 0, 0)),
        ],
        out_shape=[
            jax.ShapeDtypeStruct((nb * seq, inner), F32),
            jax.ShapeDtypeStruct((nb, SSD_GROUPS, gw, SSD_STATE), F32),
        ],
        scratch_shapes=[pltpu.VMEM((gw, SSD_STATE), F32)],
        compiler_params=_params("parallel", "parallel", "arbitrary"),
        name="ssd_scan_bwd" if reverse else "ssd_scan_fwd",
    )(*args)


def _gated_norm_body(yf_ref, yb_ref, xs_ref, z_ref, d_ref, g_ref, o_ref):
    y = yf_ref[...] + yb_ref[...] + d_ref[...] * xs_ref[...]
    z = z_ref[...]
    u = y * (z * jax.nn.sigmoid(z))
    u = u * lax.rsqrt(jnp.mean(u * u, axis=-1, keepdims=True) + EPS)
    o_ref[...] = (u * g_ref[...]).astype(o_ref.dtype)


def _gated_norm(yf, yb, xbc, zx, z_row_off, d_cols, g, inner, tr=256):
    t = yf.shape[0]
    gw = inner // SSD_GROUPS
    tr = _pick(t, tr, 8)
    zo = z_row_off // tr
    spec = pl.BlockSpec((tr, gw), lambda i, j: (i, j))
    vec = pl.BlockSpec((1, gw), lambda i, j: (0, j))
    return pl.pallas_call(
        _gated_norm_body,
        grid=(t // tr, SSD_GROUPS),
        in_specs=[spec, spec, spec, pl.BlockSpec((tr, gw), lambda i, j: (zo + i, j)), vec, vec],
        out_specs=spec,
        out_shape=jax.ShapeDtypeStruct((t, inner), BF16),
        compiler_params=_params("parallel", "parallel"),
        name="ssd_gated_norm",
    )(yf, yb, xbc, zx, d_cols.reshape(1, inner), g.reshape(1, inner))


def _moe_route(ti, n_experts, bm):
    t = ti.shape[0]
    n_slots = t * TOP_K
    flat_e = ti.reshape(-1)
    onehot = (flat_e[:, None] == jnp.arange(n_experts, dtype=jnp.int32)[None, :]).astype(jnp.int32)
    csum = jnp.cumsum(onehot, axis=0)
    counts = csum[-1]
    rank = jnp.take_along_axis(csum, flat_e[:, None], axis=1)[:, 0] - 1
    padded = (counts + bm - 1) // bm * bm
    pad_end = jnp.cumsum(padded)
    dest = (pad_end - padded)[flat_e] + rank
    n_blocks = n_slots // bm + n_experts
    slot_tok = jnp.zeros((n_blocks * bm,), jnp.int32).at[dest].set(jnp.arange(n_slots, dtype=jnp.int32) // TOP_K)
    starts = jnp.arange(n_blocks, dtype=jnp.int32) * bm
    n_used = pad_end[-1] // bm
    block_exp = jnp.minimum(jnp.sum(starts[:, None] >= pad_end[None, :], axis=1), n_experts - 1).astype(jnp.int32)
    block_exp = jnp.where(jnp.arange(n_blocks) < n_used, block_exp, block_exp[jnp.maximum(n_used - 1, 0)])
    return slot_tok, block_exp, n_used.astype(jnp.int32).reshape(1), dest.reshape(t, TOP_K)


def _row_copy(src_hbm, row, dst, sem):
    return pltpu.make_async_copy(src_hbm.at[pl.ds(row, 1)], dst, sem)


def _gather_rows_body(tok_ref, h_hbm, o_ref, buf, sem):
    bm, half = buf.shape

    def start(r, carry):
        _row_copy(h_hbm, tok_ref[0, 0, r], buf.at[pl.ds(r, 1)], sem).start()
        return carry

    def wait(r, carry):
        _row_copy(h_hbm, 0, buf.at[pl.ds(r, 1)], sem).wait()
        return carry

    lax.fori_loop(0, bm, start, 0, unroll=DMA_UNROLL)
    lax.fori_loop(0, bm, wait, 0, unroll=DMA_UNROLL)
    lo, hi = _unpack_bf16_pairs(buf[...])
    o_ref[:, :half] = lo.astype(o_ref.dtype)
    o_ref[:, half:] = hi.astype(o_ref.dtype)


def _gather_rows(hp, slot_tok, bm):
    half = hp.shape[1]
    nblk = slot_tok.shape[0] // bm
    return pl.pallas_call(
        _gather_rows_body,
        grid=(nblk,),
        in_specs=[
            pl.BlockSpec((1, 1, bm), lambda i: (i, 0, 0), memory_space=pltpu.SMEM),
            pl.BlockSpec(memory_space=pl.ANY),
        ],
        out_specs=pl.BlockSpec((bm, 2 * half), lambda i: (i, 0)),
        out_shape=jax.ShapeDtypeStruct((nblk * bm, 2 * half), BF16),
        scratch_shapes=[pltpu.VMEM((bm, half), jnp.uint32), pltpu.SemaphoreType.DMA(())],
        compiler_params=_params("arbitrary"),
        name="moe_gather",
    )(slot_tok.reshape(nblk, 1, bm), hp)


def _expert_changed(be_ref, m):
    return (m == 0) | (be_ref[m] != be_ref[jnp.maximum(m - 1, 0)])


def _moe_up_body(be_ref, nu_ref, x_ref, wg_ref, wu_ref, bg_ref, bu_ref, o_ref, wg_s, wu_s):
    m = pl.program_id(1)

    @pl.when(m < nu_ref[0])
    def _():
        @pl.when(_expert_changed(be_ref, m))
        def _():
            wg_s[...] = wg_ref[...].astype(BF16)
            wu_s[...] = wu_ref[...].astype(BF16)

        x = x_ref[...]
        g = jnp.dot(x, wg_s[...], preferred_element_type=F32) + bg_ref[...]
        u = jnp.dot(x, wu_s[...], preferred_element_type=F32) + bu_ref[...]
        g = jnp.minimum(g, SWIGLU_LIMIT)
        u = jnp.clip(u, -SWIGLU_LIMIT, SWIGLU_LIMIT)
        o_ref[...] = ((u + 1.0) * (g * jax.nn.sigmoid(SWIGLU_ALPHA * g))).astype(o_ref.dtype)

    @pl.when(m >= nu_ref[0])
    def _():
        o_ref[...] = jnp.zeros_like(o_ref)


def _moe_up(xs, w, b, layer, block_exp, n_used, bm, tn=256):
    cap, d = xs.shape
    e, f2 = b.shape[1], b.shape[2]
    f = f2 // 2
    tn = _pick(f, tn)
    nf = f // tn
    b4 = b.reshape(b.shape[0], e, 1, f2)
    wspec = lambda off: pl.BlockSpec((None, None, d, tn), lambda j, m, be, nu: (layer, be[m], 0, off + j))
    bspec = lambda off: pl.BlockSpec((None, None, 1, tn), lambda j, m, be, nu: (layer, be[m], 0, off + j))
    return pl.pallas_call(
        _moe_up_body,
        grid_spec=pltpu.PrefetchScalarGridSpec(
            num_scalar_prefetch=2,
            grid=(nf, cap // bm),
            in_specs=[pl.BlockSpec((bm, d), lambda j, m, be, nu: (m, 0)), wspec(0), wspec(nf), bspec(0), bspec(nf)],
            out_specs=pl.BlockSpec((bm, tn), lambda j, m, be, nu: (m, j)),
            scratch_shapes=[pltpu.VMEM((d, tn), BF16), pltpu.VMEM((d, tn), BF16)],
        ),
        out_shape=jax.ShapeDtypeStruct((cap, f), BF16),
        compiler_params=_params("parallel", "arbitrary"),
        name="moe_up",
    )(block_exp, n_used, xs, w, w, b4, b4)


def _moe_down_body(be_ref, nu_ref, a_ref, w_ref, b_ref, o_ref, w_s):
    m = pl.program_id(1)

    @pl.when(m < nu_ref[0])
    def _():
        @pl.when(_expert_changed(be_ref, m))
        def _():
            w_s[...] = w_ref[...].astype(BF16)

        o_ref[...] = _pack_bf16_pairs(jnp.dot(a_ref[...], w_s[...], preferred_element_type=F32) + b_ref[...])

    @pl.when(m >= nu_ref[0])
    def _():
        o_ref[...] = jnp.zeros_like(o_ref)


def _moe_down(act, w, b, layer, block_exp, n_used, bm, tn):
    cap, f = act.shape
    e, d = b.shape[1], b.shape[2]
    b4 = b.reshape(b.shape[0], e, 1, d)
    return pl.pallas_call(
        _moe_down_body,
        grid_spec=pltpu.PrefetchScalarGridSpec(
            num_scalar_prefetch=2,
            grid=(d // tn, cap // bm),
            in_specs=[
                pl.BlockSpec((bm, f), lambda j, m, be, nu: (m, 0)),
                pl.BlockSpec((None, None, f, tn), lambda j, m, be, nu: (layer, be[m], 0, j)),
                pl.BlockSpec((None, None, 1, tn), lambda j, m, be, nu: (layer, be[m], 0, j)),
            ],
            out_specs=pl.BlockSpec((bm, tn // 2), lambda j, m, be, nu: (m, j)),
            scratch_shapes=[pltpu.VMEM((f, tn), BF16)],
        ),
        out_shape=jax.ShapeDtypeStruct((cap, d // 2), jnp.uint32),
        compiler_params=_params("parallel", "arbitrary"),
        name="moe_down",
    )(block_exp, n_used, act, w, b4)


def _moe_combine_body(pos_ref, x_ref, tg_ref, g2_ref, y_hbm, o_ref, buf, sem, *, tn):
    tr = x_ref.shape[0]

    def start(r, carry):
        for k in range(TOP_K):
            _row_copy(y_hbm, pos_ref[0, 0, r * TOP_K + k], buf.at[k, pl.ds(r, 1)], sem).start()
        return carry

    def wait(r, carry):
        for k in range(TOP_K):
            _row_copy(y_hbm, 0, buf.at[k, pl.ds(r, 1)], sem).wait()
        return carry

    lax.fori_loop(0, tr, start, 0, unroll=DMA_UNROLL // TOP_K)
    lax.fori_loop(0, tr, wait, 0, unroll=DMA_UNROLL // TOP_K)
    tg = tg_ref[...]
    pw = tn // 2
    for j in range(x_ref.shape[1] // tn):
        lo = hi = None
        for k in range(TOP_K):
            lk, hk = _unpack_bf16_pairs(buf[k, :, j * pw:(j + 1) * pw])
            gk = tg[:, k:k + 1]
            lo = lk * gk if lo is None else lo + lk * gk
            hi = hk * gk if hi is None else hi + hk * gk
        for part, y in ((slice(j * tn, j * tn + pw), lo), (slice(j * tn + pw, (j + 1) * tn), hi)):
            o_ref[:, part] = x_ref[:, part] + g2_ref[0, :, part] * y


def _moe_combine(x, y_slots, pos, tg, modrows, gate_col, tn, tr=128):
    t, d = x.shape
    rows = t // modrows.shape[0]
    tr = _pick(rows, tr, 8)
    per = rows // tr
    return pl.pallas_call(
        functools.partial(_moe_combine_body, tn=tn),
        grid=(t // tr,),
        in_specs=[
            pl.BlockSpec((1, 1, tr * TOP_K), lambda i: (i, 0, 0), memory_space=pltpu.SMEM),
            pl.BlockSpec((tr, d), lambda i: (i, 0)),
            pl.BlockSpec((tr, V7X_LANES), lambda i: (i, 0)),
            pl.BlockSpec((1, 1, d), lambda i: (i // per, 0, gate_col)),
            pl.BlockSpec(memory_space=pl.ANY),
        ],
        out_specs=pl.BlockSpec((tr, d), lambda i: (i, 0)),
        out_shape=jax.ShapeDtypeStruct((t, d), F32),
        scratch_shapes=[pltpu.VMEM((TOP_K, tr, d // 2), jnp.uint32), pltpu.SemaphoreType.DMA(())],
        compiler_params=_params("arbitrary"),
        name="moe_combine",
    )(pos.reshape(t // tr, 1, tr * TOP_K), x, tg, modrows, y_slots)


def _moe_ffn(x, g, modrows, layer, router_w, router_b, w_gate_up, b_gate_up, w_down, b_down):
    n_experts = router_w.shape[-1]
    bm = MOE_BLOCK_ROWS
    tn_down = _pick(x.shape[1], MOE_DOWN_COLS)
    hp, ti, tg = _modulate_router(x, g, modrows, 3, router_w[layer], router_b[layer])
    slot_tok, block_exp, n_used, pos = _moe_route(ti[:, :TOP_K], n_experts, bm)
    xs = _gather_rows(hp, slot_tok, bm)
    act = _moe_up(xs, w_gate_up, b_gate_up, layer, block_exp, n_used, bm)
    y_slots = _moe_down(act, w_down, b_down, layer, block_exp, n_used, bm, tn_down)
    return _moe_combine(x, y_slots, pos, tg, modrows, 5, tn_down)


def kernel(x_prompt, x_sample, cache_attn_k, cache_attn_v, state_ssd_fwd, state_ssd_bwd, c, c_ctx, w_ada, b_ada, norm1_g, norm2_g, fnet_w_out, fnet_b_out, attn_w_qkv, attn_q_gain, attn_k_gain, attn_sink, attn_w_out, ssd_w_in, ssd_conv_w, ssd_conv_b, ssd_dt_bias, ssd_a_log, ssd_d, ssd_norm_g, ssd_w_out, moe_router_w, moe_router_b, moe_w_gate_up, moe_b_gate_up, moe_w_down, moe_b_down):
    bp, lp, d = x_prompt.shape
    bs, ls, _ = x_sample.shape
    tp, ts = bp * lp, bs * ls
    t = tp + ts
    depth = w_ada.shape[0]
    rows = math.gcd(lp, ls)
    x = jnp.concatenate([x_prompt.reshape(tp, d), x_sample.reshape(ts, d)], axis=0)

    assert 1 + bs <= COND_ROWS
    cond = jnp.zeros((COND_ROWS, d), F32).at[0].set(c_ctx).at[1:1 + bs].set(c)
    mod = _matmul_wbatched(cond, w_ada, b_ada, silu_in=True, tm=COND_ROWS, tn=1024, tk=2048)
    group_cond = jnp.concatenate([jnp.zeros((tp // rows,), jnp.int32),
                                  1 + jnp.arange(ts // rows, dtype=jnp.int32) // (ls // rows)])

    q_dim, kv_dim = N_HEADS * HEAD_DIM, N_KV * HEAD_DIM
    fnet_w = fnet_w_out.astype(BF16)
    new_k, new_v, new_f, new_b = [], [], [], []
    for l in range(depth):
        kind, j = l % N_MIXERS, l // N_MIXERS
        modrows = mod[l][group_cond].reshape(t // rows, 1, 6 * d)
        h = _modulate(x, norm1_g[l], modrows, 0)
        if kind == 0:
            fa, fb = _fnet_channel_dft(h)
            mixed = jnp.concatenate([_fnet_seq_dft(fa, fb, 0, bp, lp), _fnet_seq_dft(fa, fb, tp, bs, ls)], axis=0)
            x = _matmul(mixed, fnet_w, w_lead=(j,), bias=fnet_b_out[j], res=x, gate=modrows, gate_col=2,
                        name="fnet_out")
        elif kind == 1:
            qkv = _matmul(h, attn_w_qkv.astype(BF16), w_lead=(j,), tm=2048, name="attn_qkv")
            cos_t, sin_t = _rope_tables(ls, rows)
            n_tab = ls // rows
            tab = lambda i: jnp.where(i < tp // rows, n_tab, (i - tp // rows) % n_tab)
            qn = _qknorm_rope(qkv, attn_q_gain[j], cos_t, sin_t, 0, q_dim, rows, tab, BF16)
            kn = _qknorm_rope(qkv, attn_k_gain[j], cos_t, sin_t, q_dim, kv_dim, rows, tab, F32)
            new_k.append(kn[:tp].reshape(bp, lp, N_KV, HEAD_DIM))
            new_v.append(qkv[:tp, q_dim + kv_dim:].reshape(bp, lp, N_KV, HEAD_DIM))
            op = _attention(qn, kn, qkv, q_dim + kv_dim, attn_sink[j], None, None, 0, bp, lp, False)
            kc = cache_attn_k[:, j].reshape(bs, -1, kv_dim)
            vc = cache_attn_v[:, j].reshape(bs, -1, kv_dim)
            os_ = _attention(qn, kn, qkv, q_dim + kv_dim, attn_sink[j], kc, vc, tp, bs, ls, True)
            o = jnp.concatenate([op, os_], axis=0)
            x = _matmul(o, attn_w_out.astype(BF16), w_lead=(j,), res=x, gate=modrows, gate_col=2, name="attn_out")
        else:
            inner = ssd_w_out.shape[1]
            heads = inner // SSD_HEADDIM
            hpg = heads // SSD_GROUPS
            gn = SSD_GROUPS * SSD_STATE
            n_zx = 2 * inner + 2 * gn
            tn_zx = _pick(n_zx, 1024)
            w_in = ssd_w_in.astype(BF16)
            zx = _matmul(h, w_in, w_lead=(j,), n_out=n_zx, tm=2048, tn=tn_zx, name="ssd_in")
            dt_raw = _matmul(h, w_in, w_lead=(j,), n_off=n_zx // (2 * heads), n_out=2 * heads, tn=2 * heads,
                             name="ssd_in_dt")
            dt_r = dt_raw.reshape(t, 2, SSD_GROUPS, hpg).transpose(1, 2, 0, 3)
            dt_t = dt_raw.reshape(t, 2, SSD_GROUPS, hpg).transpose(1, 2, 3, 0)
            d_cols = jnp.repeat(ssd_d[j], SSD_HEADDIM)
            ys = []
            for (row_off, nb, seq, s_f, s_b) in ((0, bp, lp, None, None),
                                                 (tp, bs, ls, state_ssd_fwd[:, j], state_ssd_bwd[:, j])):
                xbc = _conv_silu(zx, inner, inner + 2 * gn, ssd_conv_w[j], ssd_conv_b[j], row_off, nb, seq)
                sl = slice(row_off, row_off + nb * seq)
                h0 = [None if s is None else s.reshape(nb, SSD_GROUPS, hpg * SSD_HEADDIM, SSD_STATE) for s in (s_f, s_b)]
                yf, hf = _ssd_scan(xbc, dt_r[:, :, sl], dt_t[:, :, :, sl], ssd_dt_bias[j], ssd_a_log[j], h0[0], 0, nb, seq, inner)
                yb, hb = _ssd_scan(xbc, dt_r[:, :, sl], dt_t[:, :, :, sl], ssd_dt_bias[j], ssd_a_log[j], h0[1], 1, nb, seq, inner)
                if s_f is None:
                    new_f.append(hf.reshape(nb, heads, SSD_HEADDIM, SSD_STATE))
                    new_b.append(hb.reshape(nb, heads, SSD_HEADDIM, SSD_STATE))
                ys.append(_gated_norm(yf, yb, xbc, zx, row_off, d_cols, ssd_norm_g[j], inner))
            x = _matmul(jnp.concatenate(ys, axis=0), ssd_w_out.astype(BF16), w_lead=(j,), res=x, gate=modrows,
                        gate_col=2, name="ssd_out")
        x = _moe_ffn(x, norm2_g[l], modrows, l, moe_router_w, moe_router_b,
                     moe_w_gate_up, moe_b_gate_up, moe_w_down, moe_b_down)

    return (x[:tp].reshape(bp, lp, d), x[tp:].reshape(bs, ls, d),
            jnp.stack(new_k, axis=1), jnp.stack(new_v, axis=1),
            jnp.stack(new_f, axis=1), jnp.stack(new_b, axis=1))
```

```python
import functools
import math

import jax
import jax.numpy as jnp
from jax import lax
from jax.experimental import pallas as pl
from jax.experimental.pallas import tpu as pltpu

GRID_W = 64
N_MIXERS = 3
EPS = 1e-6
FNET_GROUPS = 8
N_HEADS = 32
N_KV = 8
HEAD_DIM = 128
WINDOW = 128
BLOCK = 128
ROPE_THETA = 10000.0
NEG_INF = -1e30
SSD_HEADDIM = 64
SSD_GROUPS = 8
SSD_STATE = 128
SSD_CONV = 5
SSD_CHUNK = 128
TOP_K = 4
SWIGLU_LIMIT = 7.0
SWIGLU_ALPHA = 1.702

V7X_LANES = 128
V7X_SUBLANES = 8
V7X_VMEM_LIMIT_BYTES = 56 * 1024 * 1024
COND_ROWS = 16
MOE_BLOCK_ROWS = 512
MOE_DOWN_COLS = 2048
DMA_UNROLL = 8

F32 = jnp.float32
BF16 = jnp.bfloat16


def _params(*sem):
    return pltpu.CompilerParams(dimension_semantics=sem, vmem_limit_bytes=V7X_VMEM_LIMIT_BYTES)


def _pick(dim, pref, align=V7X_LANES):
    if dim <= pref:
        return dim
    t = pref - pref % align
    while t >= align:
        if dim % t == 0:
            return t
        t -= align
    return dim


def _mm_body(*refs, nk, k_axis, silu_in, has_bias, has_res, groups):
    it = iter(refs)
    x_ref, w_ref = next(it), next(it)
    b_ref = next(it) if has_bias else None
    r_ref = next(it) if has_res else None
    g_ref = next(it) if has_res else None
    o_ref, acc_ref = next(it), next(it)
    k = pl.program_id(k_axis)

    @pl.when(k == 0)
    def _():
        acc_ref[...] = jnp.zeros_like(acc_ref)

    x = x_ref[...]
    if silu_in:
        x = x.astype(F32)
        x = x * jax.nn.sigmoid(x)
    acc_ref[...] += jnp.dot(x.astype(BF16), w_ref[...].astype(BF16), preferred_element_type=F32)

    @pl.when(k == nk - 1)
    def _():
        r = acc_ref[...]
        if has_bias:
            r = r + b_ref[...]
        if has_res:
            tm, tn = r.shape
            r = (r.reshape(groups, tm // groups, tn) * g_ref[...]).reshape(tm, tn) + r_ref[...]
        o_ref[...] = r.astype(o_ref.dtype)


def _matmul(x, w, *, w_lead=(), n_off=0, n_out=None, bias=None, res=None, gate=None, gate_col=0,
            out_dtype=F32, tm=1024, tn=1024, tk=1024, name="matmul"):
    m, kdim = x.shape
    n = w.shape[-1] if n_out is None else n_out
    tm, tn, tk = _pick(m, tm, 8), _pick(n, tn), _pick(kdim, tk)
    nk = kdim // tk
    lead = tuple(w_lead)
    in_specs = [
        pl.BlockSpec((tm, tk), lambda i, j, k: (i, k)),
        pl.BlockSpec((None,) * len(lead) + (tk, tn), lambda i, j, k: lead + (k, j + n_off)),
    ]
    args = [x, w]
    if bias is not None:
        in_specs.append(pl.BlockSpec((1, tn), lambda i, j, k: (0, j)))
        args.append(bias.reshape(1, n))
    groups = 1
    if res is not None:
        rows = m // gate.shape[0]
        assert tm % rows == 0
        groups = tm // rows
        ncol = n // tn
        in_specs.append(pl.BlockSpec((tm, tn), lambda i, j, k: (i, j)))
        in_specs.append(pl.BlockSpec((groups, 1, tn), lambda i, j, k: (i, 0, gate_col * ncol + j)))
        args += [res, gate]
    body = functools.partial(_mm_body, nk=nk, k_axis=2, silu_in=False, has_bias=bias is not None,
                             has_res=res is not None, groups=groups)
    return pl.pallas_call(
        body,
        grid=(m // tm, n // tn, nk),
        in_specs=in_specs,
        out_specs=pl.BlockSpec((tm, tn), lambda i, j, k: (i, j)),
        out_shape=jax.ShapeDtypeStruct((m, n), out_dtype),
        scratch_shapes=[pltpu.VMEM((tm, tn), F32)],
        compiler_params=_params("parallel", "parallel", "arbitrary"),
        name=name,
    )(*args)


def _matmul_wbatched(x, w, bias=None, *, silu_in=False, out_dtype=F32, tm=1024, tn=1024, tk=512):
    m, kdim = x.shape
    nb, _, n = w.shape
    tm, tn, tk = _pick(m, tm, 8), _pick(n, tn), _pick(kdim, tk)
    nk = kdim // tk
    in_specs = [
        pl.BlockSpec((tm, tk), lambda b, i, j, k: (i, k)),
        pl.BlockSpec((None, tk, tn), lambda b, i, j, k: (b, k, j)),
    ]
    args = [x, w]
    if bias is not None:
        in_specs.append(pl.BlockSpec((None, 1, tn), lambda b, i, j, k: (b, 0, j)))
        args.append(bias.reshape(nb, 1, n))
    body = functools.partial(_mm_body, nk=nk, k_axis=3, silu_in=silu_in, has_bias=bias is not None,
                             has_res=False, groups=1)
    return pl.pallas_call(
        body,
        grid=(nb, m // tm, n // tn, nk),
        in_specs=in_specs,
        out_specs=pl.BlockSpec((None, tm, tn), lambda b, i, j, k: (b, i, j)),
        out_shape=jax.ShapeDtypeStruct((nb, m, n), out_dtype),
        scratch_shapes=[pltpu.VMEM((tm, tn), F32)],
        compiler_params=_params("parallel", "parallel", "parallel", "arbitrary"),
        name="adaln",
    )(*args)


def _modulated(x, g_ref, sh_ref, sc_ref):
    y = x * lax.rsqrt(jnp.mean(x * x, axis=-1, keepdims=True) + EPS) * g_ref[...]
    return y * (1.0 + sc_ref[0]) + sh_ref[0]


def _modulate_body(x_ref, g_ref, sh_ref, sc_ref, o_ref):
    o_ref[...] = _modulated(x_ref[...], g_ref, sh_ref, sc_ref).astype(o_ref.dtype)


def _mod_specs(rows, d, col):
    return [
        pl.BlockSpec((rows, d), lambda i: (i, 0)),
        pl.BlockSpec((1, d), lambda i: (0, 0)),
        pl.BlockSpec((1, 1, d), lambda i: (i, 0, col)),
        pl.BlockSpec((1, 1, d), lambda i: (i, 0, col + 1)),
    ]


def _modulate(x, g, modrows, col):
    t, d = x.shape
    rows = t // modrows.shape[0]
    return pl.pallas_call(
        _modulate_body,
        grid=(t // rows,),
        in_specs=_mod_specs(rows, d, col),
        out_specs=pl.BlockSpec((rows, d), lambda i: (i, 0)),
        out_shape=jax.ShapeDtypeStruct((t, d), BF16),
        compiler_params=_params("parallel"),
        name="modulate",
    )(x, g.reshape(1, d), modrows, modrows)


def _pack_bf16_pairs(x):
    n = x.shape[1] // 2
    lo = pltpu.bitcast(x[:, :n].astype(BF16).astype(F32), jnp.uint32)
    hi = pltpu.bitcast(x[:, n:].astype(BF16).astype(F32), jnp.uint32)
    return (hi & jnp.uint32(0xFFFF0000)) | (lo >> 16)


def _unpack_bf16_pairs(u):
    return pltpu.bitcast(u << 16, F32), pltpu.bitcast(u & jnp.uint32(0xFFFF0000), F32)


def _modulate_router_body(x_ref, g_ref, sh_ref, sc_ref, rw_ref, rb_ref, h_ref, ti_ref, tg_ref):
    h = _modulated(x_ref[...], g_ref, sh_ref, sc_ref)
    h_ref[...] = _pack_bf16_pairs(h)
    logits = jnp.dot(h, rw_ref[...], precision=lax.Precision.HIGHEST, preferred_element_type=F32) + rb_ref[...]
    lane = lax.broadcasted_iota(jnp.int32, logits.shape, 1).astype(F32)
    n_lane = float(logits.shape[1])
    vals, idxs = [], []
    for _ in range(TOP_K):
        m = jnp.max(logits, axis=-1, keepdims=True)
        idx = jnp.min(jnp.where(logits == m, lane, n_lane), axis=-1, keepdims=True)
        vals.append(m)
        idxs.append(idx)
        logits = jnp.where(lane == idx, -jnp.inf, logits)
    es = [jnp.exp(v - vals[0]) for v in vals]
    inv = 1.0 / functools.reduce(lambda a, b: a + b, es)
    ti = jnp.zeros(lane.shape, jnp.int32)
    tg = jnp.zeros(lane.shape, F32)
    for k in range(TOP_K):
        ti = jnp.where(lane == k, idxs[k].astype(jnp.int32), ti)
        tg = jnp.where(lane == k, es[k] * inv, tg)
    ti_ref[...] = ti
    tg_ref[...] = tg


def _modulate_router(x, g, modrows, col, rw, rb):
    t, d = x.shape
    rows = t // modrows.shape[0]
    e = rw.shape[1]
    rw_p = jnp.zeros((d, V7X_LANES), F32).at[:, :e].set(rw)
    rb_p = jnp.full((1, V7X_LANES), NEG_INF, F32).at[0, :e].set(rb)
    return pl.pallas_call(
        _modulate_router_body,
        grid=(t // rows,),
        in_specs=_mod_specs(rows, d, col) + [
            pl.BlockSpec((d, V7X_LANES), lambda i: (0, 0)),
            pl.BlockSpec((1, V7X_LANES), lambda i: (0, 0)),
        ],
        out_specs=[
            pl.BlockSpec((rows, d // 2), lambda i: (i, 0)),
            pl.BlockSpec((rows, V7X_LANES), lambda i: (i, 0)),
            pl.BlockSpec((rows, V7X_LANES), lambda i: (i, 0)),
        ],
        out_shape=[
            jax.ShapeDtypeStruct((t, d // 2), jnp.uint32),
            jax.ShapeDtypeStruct((t, V7X_LANES), jnp.int32),
            jax.ShapeDtypeStruct((t, V7X_LANES), F32),
        ],
        compiler_params=_params("parallel"),
        name="modulate_router",
    )(x, g.reshape(1, d), modrows, modrows, rw_p, rb_p)


def _dft_tables(n):
    j = jnp.arange(n, dtype=jnp.int32)
    ang = ((j[:, None] * j[None, :]) % n).astype(F32) * (2.0 * math.pi / n)
    s = 1.0 / math.sqrt(n)
    return jnp.cos(ang) * s, jnp.sin(ang) * s


def _fnet_channel_body(x_ref, w_ref, a_ref, b_ref):
    c = a_ref.shape[1]
    r = jnp.dot(x_ref[...], w_ref[...], preferred_element_type=F32)
    a_ref[...] = r[:, :c].astype(a_ref.dtype)
    b_ref[...] = r[:, c:].astype(b_ref.dtype)


def _fnet_channel_dft(h, tm=1024):
    t, d = h.shape
    c = d // FNET_GROUPS
    cos_c, sin_c = _dft_tables(c)
    cs = jnp.concatenate([cos_c, sin_c], axis=1).astype(BF16)
    tm = _pick(t, tm, 8)
    spec = pl.BlockSpec((tm, c), lambda i, g: (i, g))
    return pl.pallas_call(
        _fnet_channel_body,
        grid=(t // tm, FNET_GROUPS),
        in_specs=[spec, pl.BlockSpec((c, 2 * c), lambda i, g: (0, 0))],
        out_specs=[spec, spec],
        out_shape=[jax.ShapeDtypeStruct((t, d), BF16)] * 2,
        compiler_params=_params("parallel", "parallel"),
        name="fnet_channel_dft",
    )(h, cs)


def _fnet_seq_body(c_ref, s_ref, a_ref, b_ref, o_ref, acc_ref, *, nk):
    k = pl.program_id(3)

    @pl.when(k == 0)
    def _():
        acc_ref[...] = jnp.zeros_like(acc_ref)

    acc_ref[...] += (jnp.dot(c_ref[...], a_ref[...], preferred_element_type=F32)
                     + jnp.dot(s_ref[...], b_ref[...], preferred_element_type=F32))

    @pl.when(k == nk - 1)
    def _():
        o_ref[...] = acc_ref[...].astype(o_ref.dtype)


def _fnet_seq_dft(a, b, row_off, nb, seq, tm=1024, tn=1024, tk=1024):
    d = a.shape[1]
    cos_l, sin_l = _dft_tables(seq)
    cos_l, nsin_l = cos_l.astype(BF16), (-sin_l).astype(BF16)
    tm, tn, tk = _pick(seq, tm, 8), _pick(d, tn), _pick(seq, tk)
    nk, mt = seq // tk, seq // tm
    ro = row_off // tk
    tab = pl.BlockSpec((tm, tk), lambda bb, i, j, k: (i, k))
    src = pl.BlockSpec((tk, tn), lambda bb, i, j, k: (ro + bb * nk + k, j))
    return pl.pallas_call(
        functools.partial(_fnet_seq_body, nk=nk),
        grid=(nb, mt, d // tn, nk),
        in_specs=[tab, tab, src, src],
        out_specs=pl.BlockSpec((tm, tn), lambda bb, i, j, k: (bb * mt + i, j)),
        out_shape=jax.ShapeDtypeStruct((nb * seq, d), BF16),
        scratch_shapes=[pltpu.VMEM((tm, tn), F32)],
        compiler_params=_params("parallel", "parallel", "parallel", "arbitrary"),
        name="fnet_seq_dft",
    )(cos_l, nsin_l, a, b)


def _rope_tables(seq, rows_identity):
    half = HEAD_DIM // 2
    inv_freq = ROPE_THETA ** (-jnp.arange(0, half, 2, dtype=F32) / half)
    rows = seq // GRID_W
    row_pos = jnp.repeat(jnp.arange(rows, dtype=jnp.int32), GRID_W).astype(F32)
    col_pos = jnp.tile(jnp.arange(GRID_W, dtype=jnp.int32), rows).astype(F32)
    ar = row_pos[:, None] * inv_freq
    ac = col_pos[:, None] * inv_freq
    cos_t = jnp.concatenate([jnp.cos(ar), jnp.cos(ar), jnp.cos(ac), jnp.cos(ac)], axis=-1)
    sin_t = jnp.concatenate([-jnp.sin(ar), jnp.sin(ar), -jnp.sin(ac), jnp.sin(ac)], axis=-1)
    cos_t = jnp.concatenate([cos_t, jnp.ones((rows_identity, HEAD_DIM), F32)], axis=0)
    sin_t = jnp.concatenate([sin_t, jnp.zeros((rows_identity, HEAD_DIM), F32)], axis=0)
    return cos_t, sin_t


def _qknorm_rope_body(x_ref, g_ref, cos_ref, sin_ref, o_ref):
    cos_t, sin_t, gain = cos_ref[...], sin_ref[...], g_ref[...]
    lane = lax.broadcasted_iota(jnp.int32, cos_t.shape, 1)
    first = (lane % (HEAD_DIM // 2)) < (HEAD_DIM // 4)
    for h in range(x_ref.shape[1] // HEAD_DIM):
        sl = slice(h * HEAD_DIM, (h + 1) * HEAD_DIM)
        x = x_ref[:, sl]
        y = x * lax.rsqrt(jnp.mean(x * x, axis=-1, keepdims=True) + EPS) * gain
        partner = jnp.where(first, pltpu.roll(y, HEAD_DIM - HEAD_DIM // 4, 1), pltpu.roll(y, HEAD_DIM // 4, 1))
        o_ref[:, sl] = (y * cos_t + partner * sin_t).astype(o_ref.dtype)


def _qknorm_rope(qkv, gain, cos_t, sin_t, col_off, n_cols, rows, tab_index, out_dtype, tc=1024):
    t = qkv.shape[0]
    tc = _pick(n_cols, tc)
    off = col_off // tc
    return pl.pallas_call(
        _qknorm_rope_body,
        grid=(t // rows, n_cols // tc),
        in_specs=[
            pl.BlockSpec((rows, tc), lambda i, j: (i, off + j)),
            pl.BlockSpec((1, HEAD_DIM), lambda i, j: (0, 0)),
            pl.BlockSpec((rows, HEAD_DIM), lambda i, j: (tab_index(i), 0)),
            pl.BlockSpec((rows, HEAD_DIM), lambda i, j: (tab_index(i), 0)),
        ],
        out_specs=pl.BlockSpec((rows, tc), lambda i, j: (i, j)),
        out_shape=jax.ShapeDtypeStruct((t, n_cols), out_dtype),
        compiler_params=_params("parallel", "parallel"),
        name="qknorm_rope",
    )(qkv, gain.reshape(1, HEAD_DIM), cos_t, sin_t)


def _attn_body(sink_ref, q_ref, *refs, n_win):
    kw, vw = refs[:n_win], refs[n_win:2 * n_win]
    rest = refs[2 * n_win:]
    bias_ref = rest[0] if n_win else None
    kc_ref, vc_ref, o_ref = rest[-3:]
    g = pl.program_id(1)
    rep = q_ref.shape[1] // HEAD_DIM
    nq = q_ref.shape[0]
    scale = HEAD_DIM ** -0.5
    nt = (((1,), (1,)), ((), ()))
    q = jnp.concatenate([q_ref[:, r * HEAD_DIM:(r + 1) * HEAD_DIM] for r in range(rep)], axis=0).astype(BF16)
    sink = jnp.concatenate([jnp.full((nq, 1), sink_ref[g * rep + r], F32) for r in range(rep)], axis=0)
    s_c = lax.dot_general(q, kc_ref[...].astype(BF16), nt, preferred_element_type=F32) * scale
    m = jnp.maximum(jnp.max(s_c, axis=-1, keepdims=True), sink)
    if n_win:
        keys = jnp.concatenate([r[...].astype(BF16) for r in kw], axis=0)
        vals = jnp.concatenate([r[...].astype(BF16) for r in vw], axis=0)
        s_w = lax.dot_general(q, keys, nt, preferred_element_type=F32) * scale
        nk = s_w.shape[1]
        s_w = (s_w.reshape(rep, nq, nk) + bias_ref[...]).reshape(rep * nq, nk)
        m = jnp.maximum(m, jnp.max(s_w, axis=-1, keepdims=True))
    p_c = jnp.exp(s_c - m)
    denom = jnp.sum(p_c, axis=-1, keepdims=True) + jnp.exp(sink - m)
    if n_win:
        p_w = jnp.exp(s_w - m)
        denom = denom + jnp.sum(p_w, axis=-1, keepdims=True)
    inv = 1.0 / denom
    o = jnp.dot((p_c * inv).astype(BF16), vc_ref[...].astype(BF16), preferred_element_type=F32)
    if n_win:
        o = o + jnp.dot((p_w * inv).astype(BF16), vals, preferred_element_type=F32)
    for r in range(rep):
        o_ref[:, r * HEAD_DIM:(r + 1) * HEAD_DIM] = o[r * nq:(r + 1) * nq].astype(o_ref.dtype)


def _window_bias(seq):
    nblk = seq // BLOCK
    qi = jnp.arange(BLOCK, dtype=jnp.int32)[:, None]
    kj = jnp.arange(3 * BLOCK, dtype=jnp.int32)[None, :]

    def one(i):
        kpos = i * BLOCK - BLOCK + kj
        band = (jnp.abs(i * BLOCK + qi - kpos) <= WINDOW) & (kpos >= 0) & (kpos < seq)
        return jnp.where(band, 0.0, NEG_INF).astype(F32)

    return jnp.stack([one(0), one(min(1, nblk - 1)), one(nblk - 1)])


def _attention(qn, kn, qkv, v_col_off, sink, kc, vc, row_off, nb, seq, windowed):
    rep = N_HEADS // N_KV
    qw = rep * HEAD_DIM
    vo = v_col_off // HEAD_DIM
    if windowed:
        nblk = seq // BLOCK
        ro = row_off // BLOCK
        lc = kc.shape[1]

        def win(shift, col):
            return pl.BlockSpec(
                (BLOCK, HEAD_DIM),
                lambda b, g, i, s: (ro + b * nblk + jnp.clip(i + shift, 0, nblk - 1), col + g))

        in_specs = [pl.BlockSpec((BLOCK, qw), lambda b, g, i, s: (ro + b * nblk + i, g))]
        in_specs += [win(-1, 0), win(0, 0), win(1, 0), win(-1, vo), win(0, vo), win(1, vo)]
        in_specs.append(pl.BlockSpec(
            (None, BLOCK, 3 * BLOCK),
            lambda b, g, i, s: (jnp.where(i == 0, 0, jnp.where(i == nblk - 1, 2, 1)), 0, 0)))
        in_specs += [pl.BlockSpec((None, lc, HEAD_DIM), lambda b, g, i, s: (b, 0, g))] * 2
        args = [qn, kn, kn, kn, qkv, qkv, qkv, _window_bias(seq), kc, vc]
        grid = (nb, N_KV, nblk)
        out_spec = pl.BlockSpec((BLOCK, qw), lambda b, g, i, s: (b * nblk + i, g))
        body = functools.partial(_attn_body, n_win=3)
    else:
        ro = row_off // seq
        in_specs = [
            pl.BlockSpec((seq, qw), lambda b, g, i, s: (ro + b, g)),
            pl.BlockSpec((seq, HEAD_DIM), lambda b, g, i, s: (ro + b, g)),
            pl.BlockSpec((seq, HEAD_DIM), lambda b, g, i, s: (ro + b, vo + g)),
        ]
        args = [qn, kn, qkv]
        grid = (nb, N_KV, 1)
        out_spec = pl.BlockSpec((seq, qw), lambda b, g, i, s: (b, g))
        body = functools.partial(_attn_body, n_win=0)
    return pl.pallas_call(
        body,
        grid_spec=pltpu.PrefetchScalarGridSpec(
            num_scalar_prefetch=1, grid=grid, in_specs=in_specs, out_specs=out_spec),
        out_shape=jax.ShapeDtypeStruct((nb * seq, N_HEADS * HEAD_DIM), BF16),
        compiler_params=_params("parallel", "parallel", "parallel"),
        name="attention_window" if windowed else "attention_context",
    )(sink.astype(F32), *args)


def _conv_silu_body(x_ref, w_ref, b_ref, o_ref):
    x = x_ref[...]
    seq = x.shape[0]
    pad = SSD_CONV // 2
    t = lax.broadcasted_iota(jnp.int32, (seq, 1), 0)
    acc = b_ref[...] + x * w_ref[pad:pad + 1, :]
    for k in range(SSD_CONV):
        s = k - pad
        if s == 0:
            continue
        shifted = pltpu.roll(x, (-s) % seq, 0)
        acc = acc + jnp.where((t + s >= 0) & (t + s < seq), shifted, 0.0) * w_ref[k:k + 1, :]
    o_ref[...] = acc * jax.nn.sigmoid(acc)


def _conv_silu(zx, col_off, n_cols, w, b, row_off, nb, seq, tc=256):
    tc = _pick(n_cols, tc)
    co, ro = col_off // tc, row_off // seq
    return pl.pallas_call(
        _conv_silu_body,
        grid=(nb, n_cols // tc),
        in_specs=[
            pl.BlockSpec((seq, tc), lambda b, j: (ro + b, co + j)),
            pl.BlockSpec((SSD_CONV, tc), lambda b, j: (0, j)),
            pl.BlockSpec((1, tc), lambda b, j: (0, j)),
        ],
        out_specs=pl.BlockSpec((seq, tc), lambda b, j: (b, j)),
        out_shape=jax.ShapeDtypeStruct((nb * seq, n_cols), F32),
        compiler_params=_params("parallel", "parallel"),
        name="ssd_conv_silu",
    )(zx, w, b.reshape(1, n_cols))


def _softplus(x):
    return jnp.maximum(x, 0.0) + jnp.log(1.0 + jnp.exp(-jnp.abs(x)))


def _ssd_scan_body(x_ref, b_ref, c_ref, dt_ref, dtt_ref, db_ref, dbt_ref, al_ref, alt_ref, *refs,
                   nc, reverse, has_h0):
    if has_h0:
        h0_ref, y_ref, hf_ref, h_ref = refs
    else:
        y_ref, hf_ref, h_ref = refs
    c_idx = pl.program_id(2)
    q = x_ref.shape[0]
    hpg = dt_ref.shape[1]
    p = x_ref.shape[1] // hpg

    @pl.when(c_idx == 0)
    def _():
        h_ref[...] = h0_ref[...] if has_h0 else jnp.zeros_like(h_ref)

    hi = lax.Precision.HIGHEST
    ii = lax.broadcasted_iota(jnp.int32, (q, q), 0)
    jj = lax.broadcasted_iota(jnp.int32, (q, q), 1)
    incl = (jj >= ii) if reverse else (jj <= ii)
    inclf = incl.astype(F32)
    dt = _softplus(dt_ref[...] + db_ref[...])
    dtt = _softplus(dtt_ref[...] + dbt_ref[...])
    da = dt * (-jnp.exp(al_ref[...]))
    dat = dtt * (-jnp.exp(alt_ref[...]))
    acum = jnp.dot(inclf, da, precision=hi, preferred_element_type=F32)
    acumt = lax.dot_general(dat, inclf, (((1,), (1,)), ((), ())), precision=hi,
                            preferred_element_type=F32)
    total = jnp.sum(da, axis=0, keepdims=True)
    total_t = jnp.sum(dat, axis=1, keepdims=True)
    xb = x_ref[...]
    bm = b_ref[...].astype(BF16)
    cm = c_ref[...].astype(BF16)
    cb = lax.dot_general(cm, bm, (((1,), (1,)), ((), ())), preferred_element_type=F32)
    h = h_ref[...]
    ch = lax.dot_general(cm, h.astype(BF16), (((1,), (1,)), ((), ())), preferred_element_type=F32)
    lane = lax.broadcasted_iota(jnp.int32, (q, 2 * p), 1)
    decay = []
    for r in range(0, hpg, 2):
        sl = slice(r * p, (r + 2) * p)
        mats, ex = [], []
        for rr in (r, r + 1):
            ai = jnp.broadcast_to(acum[:, rr:rr + 1], (q, q))
            lmat = jnp.exp(jnp.where(incl, ai - acumt[rr:rr + 1, :], -jnp.inf))
            mats.append((cb * lmat * dtt[rr:rr + 1, :]).astype(BF16))
            ex.append(jnp.exp(ai[:, :2 * p]))
            decay.append(jnp.broadcast_to(jnp.exp(total[:, rr:rr + 1]), (p, 1)))
        x2 = xb[:, sl]
        rhs = jnp.concatenate([jnp.where(lane < p, x2, 0.0), jnp.where(lane >= p, x2, 0.0)], axis=0).astype(BF16)
        intra = jnp.dot(jnp.concatenate(mats, axis=1), rhs, preferred_element_type=F32)
        y_ref[:, sl] = intra + ch[:, sl] * jnp.where(lane < p, ex[0], ex[1])
    w_t = jnp.exp(total_t - acumt) * dtt
    xwt = (xb.T.reshape(hpg, p, q) * w_t[:, None, :]).reshape(hpg * p, q).astype(BF16)
    h_new = h * jnp.concatenate(decay, axis=0) + jnp.dot(xwt, bm, preferred_element_type=F32)
    h_ref[...] = h_new

    @pl.when(c_idx == nc - 1)
    def _():
        hf_ref[...] = h_new


def _ssd_scan(xbc, dt_r, dt_t, dt_bias, a_log, h0, direction, nb, seq, inner):
    nc = seq // SSD_CHUNK
    hpg = inner // SSD_HEADDIM // SSD_GROUPS
    gw = hpg * SSD_HEADDIM
    reverse = direction == 1
    bo = inner // SSD_STATE
    co = bo + SSD_GROUPS

    def tok(b, c):
        return b * nc + (nc - 1 - c if reverse else c)

    db = dt_bias.reshape(2, SSD_GROUPS, 1, hpg)
    al = a_log.reshape(2, SSD_GROUPS, 1, hpg)
    in_specs = [
        pl.BlockSpec((SSD_CHUNK, gw), lambda b, g, c: (tok(b, c), g)),
        pl.BlockSpec((SSD_CHUNK, SSD_STATE), lambda b, g, c: (tok(b, c), bo + g)),
        pl.BlockSpec((SSD_CHUNK, SSD_STATE), lambda b, g, c: (tok(b, c), co + g)),
        pl.BlockSpec((None, None, SSD_CHUNK, hpg), lambda b, g, c: (direction, g, tok(b, c), 0)),
        pl.BlockSpec((None, None, hpg, SSD_CHUNK), lambda b, g, c: (direction, g, 0, tok(b, c))),
        pl.BlockSpec((None, None, 1, hpg), lambda b, g, c: (direction, g, 0, 0)),
        pl.BlockSpec((None, None, hpg, 1), lambda b, g, c: (direction, g, 0, 0)),
        pl.BlockSpec((None, None, 1, hpg), lambda b, g, c: (direction, g, 0, 0)),
        pl.BlockSpec((None, None, hpg, 1), lambda b, g, c: (direction, g, 0, 0)),
    ]
    args = [xbc, xbc, xbc, dt_r, dt_t, db, db.reshape(2, SSD_GROUPS, hpg, 1), al, al.reshape(2, SSD_GROUPS, hpg, 1)]
    if h0 is not None:
        in_specs.append(pl.BlockSpec((None, None, gw, SSD_STATE), lambda b, g, c: (b, g, 0, 0)))
        args.append(h0)
    body = functools.partial(_ssd_scan_body, nc=nc, reverse=reverse, has_h0=h0 is not None)
    return pl.pallas_call(
        body,
        grid=(nb, SSD_GROUPS, nc),
        in_specs=in_specs,
        out_specs=[
            pl.BlockSpec((SSD_CHUNK, gw), lambda b, g, c: (tok(b, c), g)),
            pl.BlockSpec((None, None, gw, SSD_STATE), lambda b, g, c: (b, g, 0, 0)),
        ],
        out_shape=[
            jax.ShapeDtypeStruct((nb * seq, inner), F32),
            jax.ShapeDtypeStruct((nb, SSD_GROUPS, gw, SSD_STATE), F32),
        ],
        scratch_shapes=[pltpu.VMEM((gw, SSD_STATE), F32)],
        compiler_params=_params("parallel", "parallel", "arbitrary"),
        name="ssd_scan_bwd" if reverse else "ssd_scan_fwd",
    )(*args)


def _gated_norm_body(yf_ref, yb_ref, xs_ref, z_ref, d_ref, g_ref, o_ref):
    y = yf_ref[...] + yb_ref[...] + d_ref[...] * xs_ref[...]
    z = z_ref[...]
    u = y * (z * jax.nn.sigmoid(z))
    u = u * lax.rsqrt(jnp.mean(u * u, axis=-1, keepdims=True) + EPS)
    o_ref[...] = (u * g_ref[...]).astype(o_ref.dtype)


def _gated_norm(yf, yb, xbc, zx, z_row_off, d_cols, g, inner, tr=256):
    t = yf.shape[0]
    gw = inner // SSD_GROUPS
    tr = _pick(t, tr, 8)
    zo = z_row_off // tr
    spec = pl.BlockSpec((tr, gw), lambda i, j: (i, j))
    vec = pl.BlockSpec((1, gw), lambda i, j: (0, j))
    return pl.pallas_call(
        _gated_norm_body,
        grid=(t // tr, SSD_GROUPS),
        in_specs=[spec, spec, spec, pl.BlockSpec((tr, gw), lambda i, j: (zo + i, j)), vec, vec],
        out_specs=spec,
        out_shape=jax.ShapeDtypeStruct((t, inner), BF16),
        compiler_params=_params("parallel", "parallel"),
        name="ssd_gated_norm",
    )(yf, yb, xbc, zx, d_cols.reshape(1, inner), g.reshape(1, inner))


def _moe_route(ti, n_experts, bm):
    t = ti.shape[0]
    n_slots = t * TOP_K
    flat_e = ti.reshape(-1)
    onehot = (flat_e[:, None] == jnp.arange(n_experts, dtype=jnp.int32)[None, :]).astype(jnp.int32)
    csum = jnp.cumsum(onehot, axis=0)
    counts = csum[-1]
    rank = jnp.take_along_axis(csum, flat_e[:, None], axis=1)[:, 0] - 1
    padded = (counts + bm - 1) // bm * bm
    pad_end = jnp.cumsum(padded)
    dest = (pad_end - padded)[flat_e] + rank
    n_blocks = n_slots // bm + n_experts
    slot_tok = jnp.zeros((n_blocks * bm,), jnp.int32).at[dest].set(jnp.arange(n_slots, dtype=jnp.int32) // TOP_K)
    starts = jnp.arange(n_blocks, dtype=jnp.int32) * bm
    n_used = pad_end[-1] // bm
    block_exp = jnp.minimum(jnp.sum(starts[:, None] >= pad_end[None, :], axis=1), n_experts - 1).astype(jnp.int32)
    block_exp = jnp.where(jnp.arange(n_blocks) < n_used, block_exp, block_exp[jnp.maximum(n_used - 1, 0)])
    return slot_tok, block_exp, n_used.astype(jnp.int32).reshape(1), dest.reshape(t, TOP_K)


def _row_copy(src_hbm, row, dst, sem):
    return pltpu.make_async_copy(src_hbm.at[pl.ds(row, 1)], dst, sem)


def _burst_rows(n, rows):
    per_sub = rows // V7X_SUBLANES // DMA_UNROLL
    base = (n % per_sub) * (DMA_UNROLL * V7X_SUBLANES) + n // per_sub
    return [base + u * V7X_SUBLANES for u in range(DMA_UNROLL)]


def _gather_rows_body(tok_ref, nxt_ref, h_hbm, o_ref, buf, sem):
    _, bm, half = buf.shape
    i, n = pl.program_id(0), pl.num_programs(0)
    slot = i % 2

    def start_block(idx_ref, s):
        def burst(b, carry):
            for u, r in enumerate(_burst_rows(b, bm)):
                _row_copy(h_hbm, idx_ref[0, 0, r], buf.at[s, pl.ds(r, 1)], sem.at[s]).start(priority=u % 2)
            return carry

        lax.fori_loop(0, bm // DMA_UNROLL, burst, 0)

    @pl.when(i == 0)
    def _():
        start_block(tok_ref, 0)

    @pl.when(i + 1 < n)
    def _():
        start_block(nxt_ref, 1 - slot)

    def wait(r, carry):
        _row_copy(h_hbm, 0, buf.at[slot, pl.ds(r, 1)], sem.at[slot]).wait()
        return carry

    lax.fori_loop(0, bm, wait, 0, unroll=DMA_UNROLL)
    lo, hi = _unpack_bf16_pairs(buf[slot])
    o_ref[:, :half] = lo.astype(o_ref.dtype)
    o_ref[:, half:] = hi.astype(o_ref.dtype)


def _gather_rows(hp, slot_tok, bm):
    half = hp.shape[1]
    nblk = slot_tok.shape[0] // bm
    assert bm % (V7X_SUBLANES * DMA_UNROLL) == 0
    tok = slot_tok.reshape(nblk, 1, bm)
    return pl.pallas_call(
        _gather_rows_body,
        grid=(nblk,),
        in_specs=[
            pl.BlockSpec((1, 1, bm), lambda i: (i, 0, 0), memory_space=pltpu.SMEM),
            pl.BlockSpec((1, 1, bm), lambda i: (jnp.minimum(i + 1, nblk - 1), 0, 0), memory_space=pltpu.SMEM),
            pl.BlockSpec(memory_space=pl.ANY),
        ],
        out_specs=pl.BlockSpec((bm, 2 * half), lambda i: (i, 0)),
        out_shape=jax.ShapeDtypeStruct((nblk * bm, 2 * half), BF16),
        scratch_shapes=[pltpu.VMEM((2, bm, half), jnp.uint32), pltpu.SemaphoreType.DMA((2,))],
        compiler_params=_params("arbitrary"),
        name="moe_gather",
    )(tok, tok, hp)


def _expert_changed(be_ref, m):
    return (m == 0) | (be_ref[m] != be_ref[jnp.maximum(m - 1, 0)])


def _moe_up_body(be_ref, nu_ref, x_ref, wg_ref, wu_ref, bg_ref, bu_ref, o_ref, wg_s, wu_s):
    m = pl.program_id(1)

    @pl.when(m < nu_ref[0])
    def _():
        @pl.when(_expert_changed(be_ref, m))
        def _():
            wg_s[...] = wg_ref[...].astype(BF16)
            wu_s[...] = wu_ref[...].astype(BF16)

        x = x_ref[...]
        g = jnp.dot(x, wg_s[...], preferred_element_type=F32) + bg_ref[...]
        u = jnp.dot(x, wu_s[...], preferred_element_type=F32) + bu_ref[...]
        g = jnp.minimum(g, SWIGLU_LIMIT)
        u = jnp.clip(u, -SWIGLU_LIMIT, SWIGLU_LIMIT)
        o_ref[...] = ((u + 1.0) * (g * jax.nn.sigmoid(SWIGLU_ALPHA * g))).astype(o_ref.dtype)

    @pl.when(m >= nu_ref[0])
    def _():
        o_ref[...] = jnp.zeros_like(o_ref)


def _moe_up(xs, w, b, layer, block_exp, n_used, bm, tn=512):
    cap, d = xs.shape
    e, f2 = b.shape[1], b.shape[2]
    f = f2 // 2
    tn = _pick(f, tn)
    nf = f // tn
    b4 = b.reshape(b.shape[0], e, 1, f2)
    wspec = lambda off: pl.BlockSpec((None, None, d, tn), lambda j, m, be, nu: (layer, be[m], 0, off + j))
    bspec = lambda off: pl.BlockSpec((None, None, 1, tn), lambda j, m, be, nu: (layer, be[m], 0, off + j))
    return pl.pallas_call(
        _moe_up_body,
        grid_spec=pltpu.PrefetchScalarGridSpec(
            num_scalar_prefetch=2,
            grid=(nf, cap // bm),
            in_specs=[pl.BlockSpec((bm, d), lambda j, m, be, nu: (m, 0)), wspec(0), wspec(nf), bspec(0), bspec(nf)],
            out_specs=pl.BlockSpec((bm, tn), lambda j, m, be, nu: (m, j)),
            scratch_shapes=[pltpu.VMEM((d, tn), BF16), pltpu.VMEM((d, tn), BF16)],
        ),
        out_shape=jax.ShapeDtypeStruct((cap, f), BF16),
        compiler_params=_params("parallel", "arbitrary"),
        name="moe_up",
    )(block_exp, n_used, xs, w, w, b4, b4)


def _moe_down_body(be_ref, nu_ref, a_ref, w_ref, b_ref, o_ref, w_s):
    m = pl.program_id(1)

    @pl.when(m < nu_ref[0])
    def _():
        @pl.when(_expert_changed(be_ref, m))
        def _():
            w_s[...] = w_ref[...].astype(BF16)

        o_ref[...] = _pack_bf16_pairs(jnp.dot(a_ref[...], w_s[...], preferred_element_type=F32) + b_ref[...])

    @pl.when(m >= nu_ref[0])
    def _():
        o_ref[...] = jnp.zeros_like(o_ref)


def _moe_down(act, w, b, layer, block_exp, n_used, bm, tn):
    cap, f = act.shape
    e, d = b.shape[1], b.shape[2]
    b4 = b.reshape(b.shape[0], e, 1, d)
    return pl.pallas_call(
        _moe_down_body,
        grid_spec=pltpu.PrefetchScalarGridSpec(
            num_scalar_prefetch=2,
            grid=(d // tn, cap // bm),
            in_specs=[
                pl.BlockSpec((bm, f), lambda j, m, be, nu: (m, 0)),
                pl.BlockSpec((None, None, f, tn), lambda j, m, be, nu: (layer, be[m], 0, j)),
                pl.BlockSpec((None, None, 1, tn), lambda j, m, be, nu: (layer, be[m], 0, j)),
            ],
            out_specs=pl.BlockSpec((bm, tn // 2), lambda j, m, be, nu: (m, j)),
            scratch_shapes=[pltpu.VMEM((f, tn), BF16)],
        ),
        out_shape=jax.ShapeDtypeStruct((cap, d // 2), jnp.uint32),
        compiler_params=_params("parallel", "arbitrary"),
        name="moe_down",
    )(block_exp, n_used, act, w, b4)


def _moe_combine_body(pos_ref, nxt_ref, x_ref, tg_ref, g2_ref, y_hbm, o_ref, buf, sem, *, tn):
    tr = x_ref.shape[0]
    i, n = pl.program_id(0), pl.num_programs(0)
    slot = i % 2

    def start_block(idx_ref, s):
        def burst(b, carry):
            for r in _burst_rows(b, tr):
                for k in range(TOP_K):
                    _row_copy(y_hbm, idx_ref[0, 0, r * TOP_K + k], buf.at[s, k, pl.ds(r, 1)], sem.at[s]).start()
            return carry

        lax.fori_loop(0, tr // DMA_UNROLL, burst, 0)

    @pl.when(i == 0)
    def _():
        start_block(pos_ref, 0)

    @pl.when(i + 1 < n)
    def _():
        start_block(nxt_ref, 1 - slot)

    def wait(r, carry):
        for k in range(TOP_K):
            _row_copy(y_hbm, 0, buf.at[slot, k, pl.ds(r, 1)], sem.at[slot]).wait()
        return carry

    lax.fori_loop(0, tr, wait, 0, unroll=DMA_UNROLL // TOP_K)
    tg = tg_ref[...]
    pw = tn // 2
    for j in range(x_ref.shape[1] // tn):
        lo = hi = None
        for k in range(TOP_K):
            lk, hk = _unpack_bf16_pairs(buf[slot, k, :, j * pw:(j + 1) * pw])
            gk = tg[:, k:k + 1]
            lo = lk * gk if lo is None else lo + lk * gk
            hi = hk * gk if hi is None else hi + hk * gk
        for part, y in ((slice(j * tn, j * tn + pw), lo), (slice(j * tn + pw, (j + 1) * tn), hi)):
            o_ref[:, part] = x_ref[:, part] + g2_ref[0, :, part] * y


def _moe_combine(x, y_slots, pos, tg, modrows, gate_col, tn, tr=128):
    t, d = x.shape
    rows = t // modrows.shape[0]
    tr = _pick(rows, tr, 8)
    per = rows // tr
    nblk = t // tr
    assert tr % (V7X_SUBLANES * DMA_UNROLL) == 0
    pos3 = pos.reshape(nblk, 1, tr * TOP_K)
    return pl.pallas_call(
        functools.partial(_moe_combine_body, tn=tn),
        grid=(nblk,),
        in_specs=[
            pl.BlockSpec((1, 1, tr * TOP_K), lambda i: (i, 0, 0), memory_space=pltpu.SMEM),
            pl.BlockSpec((1, 1, tr * TOP_K), lambda i: (jnp.minimum(i + 1, nblk - 1), 0, 0), memory_space=pltpu.SMEM),
            pl.BlockSpec((tr, d), lambda i: (i, 0)),
            pl.BlockSpec((tr, V7X_LANES), lambda i: (i, 0)),
            pl.BlockSpec((1, 1, d), lambda i: (i // per, 0, gate_col)),
            pl.BlockSpec(memory_space=pl.ANY),
        ],
        out_specs=pl.BlockSpec((tr, d), lambda i: (i, 0)),
        out_shape=jax.ShapeDtypeStruct((t, d), F32),
        scratch_shapes=[pltpu.VMEM((2, TOP_K, tr, d // 2), jnp.uint32), pltpu.SemaphoreType.DMA((2,))],
        compiler_params=_params("arbitrary"),
        name="moe_combine",
    )(pos3, pos3, x, tg, modrows, y_slots)


def _moe_ffn(x, g, modrows, layer, router_w, router_b, w_gate_up, b_gate_up, w_down, b_down):
    n_experts = router_w.shape[-1]
    bm = MOE_BLOCK_ROWS
    tn_down = _pick(x.shape[1], MOE_DOWN_COLS)
    hp, ti, tg = _modulate_router(x, g, modrows, 3, router_w[layer], router_b[layer])
    slot_tok, block_exp, n_used, pos = _moe_route(ti[:, :TOP_K], n_experts, bm)
    xs = _gather_rows(hp, slot_tok, bm)
    act = _moe_up(xs, w_gate_up, b_gate_up, layer, block_exp, n_used, bm)
    y_slots = _moe_down(act, w_down, b_down, layer, block_exp, n_used, bm, tn_down)
    return _moe_combine(x, y_slots, pos, tg, modrows, 5, tn_down)


def kernel(x_prompt, x_sample, cache_attn_k, cache_attn_v, state_ssd_fwd, state_ssd_bwd, c, c_ctx, w_ada, b_ada, norm1_g, norm2_g, fnet_w_out, fnet_b_out, attn_w_qkv, attn_q_gain, attn_k_gain, attn_sink, attn_w_out, ssd_w_in, ssd_conv_w, ssd_conv_b, ssd_dt_bias, ssd_a_log, ssd_d, ssd_norm_g, ssd_w_out, moe_router_w, moe_router_b, moe_w_gate_up, moe_b_gate_up, moe_w_down, moe_b_down):
    bp, lp, d = x_prompt.shape
    bs, ls, _ = x_sample.shape
    tp, ts = bp * lp, bs * ls
    t = tp + ts
    depth = w_ada.shape[0]
    rows = math.gcd(lp, ls)
    x = jnp.concatenate([x_prompt.reshape(tp, d), x_sample.reshape(ts, d)], axis=0)

    assert 1 + bs <= COND_ROWS
    cond = jnp.zeros((COND_ROWS, d), F32).at[0].set(c_ctx).at[1:1 + bs].set(c)
    mod = _matmul_wbatched(cond, w_ada, b_ada, silu_in=True, tm=COND_ROWS, tn=1024, tk=2048)
    group_cond = jnp.concatenate([jnp.zeros((tp // rows,), jnp.int32),
                                  1 + jnp.arange(ts // rows, dtype=jnp.int32) // (ls // rows)])

    q_dim, kv_dim = N_HEADS * HEAD_DIM, N_KV * HEAD_DIM
    fnet_w = fnet_w_out.astype(BF16)
    new_k, new_v, new_f, new_b = [], [], [], []
    for l in range(depth):
        kind, j = l % N_MIXERS, l // N_MIXERS
        modrows = mod[l][group_cond].reshape(t // rows, 1, 6 * d)
        h = _modulate(x, norm1_g[l], modrows, 0)
        if kind == 0:
            fa, fb = _fnet_channel_dft(h)
            mixed = jnp.concatenate([_fnet_seq_dft(fa, fb, 0, bp, lp), _fnet_seq_dft(fa, fb, tp, bs, ls)], axis=0)
            x = _matmul(mixed, fnet_w, w_lead=(j,), bias=fnet_b_out[j], res=x, gate=modrows, gate_col=2,
                        name="fnet_out")
        elif kind == 1:
            qkv = _matmul(h, attn_w_qkv.astype(BF16), w_lead=(j,), tm=2048, name="attn_qkv")
            cos_t, sin_t = _rope_tables(ls, rows)
            n_tab = ls // rows
            tab = lambda i: jnp.where(i < tp // rows, n_tab, (i - tp // rows) % n_tab)
            qn = _qknorm_rope(qkv, attn_q_gain[j], cos_t, sin_t, 0, q_dim, rows, tab, BF16)
            kn = _qknorm_rope(qkv, attn_k_gain[j], cos_t, sin_t, q_dim, kv_dim, rows, tab, F32)
            new_k.append(kn[:tp].reshape(bp, lp, N_KV, HEAD_DIM))
            new_v.append(qkv[:tp, q_dim + kv_dim:].reshape(bp, lp, N_KV, HEAD_DIM))
            op = _attention(qn, kn, qkv, q_dim + kv_dim, attn_sink[j], None, None, 0, bp, lp, False)
            kc = cache_attn_k[:, j].reshape(bs, -1, kv_dim)
            vc = cache_attn_v[:, j].reshape(bs, -1, kv_dim)
            os_ = _attention(qn, kn, qkv, q_dim + kv_dim, attn_sink[j], kc, vc, tp, bs, ls, True)
            o = jnp.concatenate([op, os_], axis=0)
            x = _matmul(o, attn_w_out.astype(BF16), w_lead=(j,), res=x, gate=modrows, gate_col=2, name="attn_out")
        else:
            inner = ssd_w_out.shape[1]
            heads = inner // SSD_HEADDIM
            hpg = heads // SSD_GROUPS
            gn = SSD_GROUPS * SSD_STATE
            n_zx = 2 * inner + 2 * gn
            tn_zx = _pick(n_zx, 1024)
            w_in = ssd_w_in.astype(BF16)
            zx = _matmul(h, w_in, w_lead=(j,), n_out=n_zx, tm=2048, tn=tn_zx, name="ssd_in")
            dt_raw = _matmul(h, w_in, w_lead=(j,), n_off=n_zx // (2 * heads), n_out=2 * heads, tn=2 * heads,
                             name="ssd_in_dt")
            dt_r = dt_raw.reshape(t, 2, SSD_GROUPS, hpg).transpose(1, 2, 0, 3)
            dt_t = dt_raw.reshape(t, 2, SSD_GROUPS, hpg).transpose(1, 2, 3, 0)
            d_cols = jnp.repeat(ssd_d[j], SSD_HEADDIM)
            ys = []
            for (row_off, nb, seq, s_f, s_b) in ((0, bp, lp, None, None),
                                                 (tp, bs, ls, state_ssd_fwd[:, j], state_ssd_bwd[:, j])):
                xbc = _conv_silu(zx, inner, inner + 2 * gn, ssd_conv_w[j], ssd_conv_b[j], row_off, nb, seq)
                sl = slice(row_off, row_off + nb * seq)
                h0 = [None if s is None else s.reshape(nb, SSD_GROUPS, hpg * SSD_HEADDIM, SSD_STATE) for s in (s_f, s_b)]
                yf, hf = _ssd_scan(xbc, dt_r[:, :, sl], dt_t[:, :, :, sl], ssd_dt_bias[j], ssd_a_log[j], h0[0], 0, nb, seq, inner)
                yb, hb = _ssd_scan(xbc, dt_r[:, :, sl], dt_t[:, :, :, sl], ssd_dt_bias[j], ssd_a_log[j], h0[1], 1, nb, seq, inner)
                if s_f is None:
                    new_f.append(hf.reshape(nb, heads, SSD_HEADDIM, SSD_STATE))
                    new_b.append(hb.reshape(nb, heads, SSD_HEADDIM, SSD_STATE))
                ys.append(_gated_norm(yf, yb, xbc, zx, row_off, d_cols, ssd_norm_g[j], inner))
            x = _matmul(jnp.concatenate(ys, axis=0), ssd_w_out.astype(BF16), w_lead=(j,), res=x, gate=modrows,
                        gate_col=2, name="ssd_out")
        x = _moe_ffn(x, norm2_g[l], modrows, l, moe_router_w, moe_router_b,
                     moe_w_gate_up, moe_b_gate_up, moe_w_down, moe_b_down)

    return (x[:tp].reshape(bp, lp, d), x[tp:].reshape(bs, ls, d),
            jnp.stack(new_k, axis=1), jnp.stack(new_v, axis=1),
            jnp.stack(new_f, axis=1), jnp.stack(new_b, axis=1))
```

```python
import functools
import math

import jax
import jax.numpy as jnp
from jax import lax
from jax.experimental import pallas as pl
from jax.experimental.pallas import tpu as pltpu

GRID_W = 64
N_MIXERS = 3
EPS = 1e-6
FNET_GROUPS = 8
N_HEADS = 32
N_KV = 8
HEAD_DIM = 128
WINDOW = 128
BLOCK = 128
ROPE_THETA = 10000.0
NEG_INF = -1e30
SSD_HEADDIM = 64
SSD_GROUPS = 8
SSD_STATE = 128
SSD_CONV = 5
SSD_CHUNK = 128
TOP_K = 4
SWIGLU_LIMIT = 7.0
SWIGLU_ALPHA = 1.702

V7X_LANES = 128
V7X_SUBLANES = 8
V7X_VMEM_LIMIT_BYTES = 56 * 1024 * 1024
COND_ROWS = 16
MOE_BLOCK_ROWS = 512
MOE_DOWN_COLS = 2048
DMA_UNROLL = 8

F32 = jnp.float32
BF16 = jnp.bfloat16


def _params(*sem):
    return pltpu.CompilerParams(dimension_semantics=sem, vmem_limit_bytes=V7X_VMEM_LIMIT_BYTES)


def _pick(dim, pref, align=V7X_LANES):
    if dim <= pref:
        return dim
    t = pref - pref % align
    while t >= align:
        if dim % t == 0:
            return t
        t -= align
    return dim


def _mm_body(*refs, nk, k_axis, silu_in, has_bias, has_res, groups):
    it = iter(refs)
    x_ref, w_ref = next(it), next(it)
    b_ref = next(it) if has_bias else None
    r_ref = next(it) if has_res else None
    g_ref = next(it) if has_res else None
    o_ref, acc_ref = next(it), next(it)
    k = pl.program_id(k_axis)

    @pl.when(k == 0)
    def _():
        acc_ref[...] = jnp.zeros_like(acc_ref)

    x = x_ref[...]
    if silu_in:
        x = x.astype(F32)
        x = x * jax.nn.sigmoid(x)
    acc_ref[...] += jnp.dot(x.astype(BF16), w_ref[...].astype(BF16), preferred_element_type=F32)

    @pl.when(k == nk - 1)
    def _():
        r = acc_ref[...]
        if has_bias:
            r = r + b_ref[...]
        if has_res:
            tm, tn = r.shape
            r = (r.reshape(groups, tm // groups, tn) * g_ref[...]).reshape(tm, tn) + r_ref[...]
        o_ref[...] = r.astype(o_ref.dtype)


def _matmul(x, w, *, w_lead=(), n_off=0, n_out=None, bias=None, res=None, gate=None, gate_col=0,
            out_dtype=F32, tm=1024, tn=1024, tk=1024, name="matmul"):
    m, kdim = x.shape
    n = w.shape[-1] if n_out is None else n_out
    tm, tn, tk = _pick(m, tm, 8), _pick(n, tn), _pick(kdim, tk)
    nk = kdim // tk
    lead = tuple(w_lead)
    in_specs = [
        pl.BlockSpec((tm, tk), lambda i, j, k: (i, k)),
        pl.BlockSpec((None,) * len(lead) + (tk, tn), lambda i, j, k: lead + (k, j + n_off)),
    ]
    args = [x, w]
    if bias is not None:
        in_specs.append(pl.BlockSpec((1, tn), lambda i, j, k: (0, j)))
        args.append(bias.reshape(1, n))
    groups = 1
    if res is not None:
        rows = m // gate.shape[0]
        assert tm % rows == 0
        groups = tm // rows
        ncol = n // tn
        in_specs.append(pl.BlockSpec((tm, tn), lambda i, j, k: (i, j)))
        in_specs.append(pl.BlockSpec((groups, 1, tn), lambda i, j, k: (i, 0, gate_col * ncol + j)))
        args += [res, gate]
    body = functools.partial(_mm_body, nk=nk, k_axis=2, silu_in=False, has_bias=bias is not None,
                             has_res=res is not None, groups=groups)
    return pl.pallas_call(
        body,
        grid=(m // tm, n // tn, nk),
        in_specs=in_specs,
        out_specs=pl.BlockSpec((tm, tn), lambda i, j, k: (i, j)),
        out_shape=jax.ShapeDtypeStruct((m, n), out_dtype),
        scratch_shapes=[pltpu.VMEM((tm, tn), F32)],
        compiler_params=_params("parallel", "parallel", "arbitrary"),
        name=name,
    )(*args)


def _matmul_wbatched(x, w, bias=None, *, silu_in=False, out_dtype=F32, tm=1024, tn=1024, tk=512):
    m, kdim = x.shape
    nb, _, n = w.shape
    tm, tn, tk = _pick(m, tm, 8), _pick(n, tn), _pick(kdim, tk)
    nk = kdim // tk
    in_specs = [
        pl.BlockSpec((tm, tk), lambda b, i, j, k: (i, k)),
        pl.BlockSpec((None, tk, tn), lambda b, i, j, k: (b, k, j)),
    ]
    args = [x, w]
    if bias is not None:
        in_specs.append(pl.BlockSpec((None, 1, tn), lambda b, i, j, k: (b, 0, j)))
        args.append(bias.reshape(nb, 1, n))
    body = functools.partial(_mm_body, nk=nk, k_axis=3, silu_in=silu_in, has_bias=bias is not None,
                             has_res=False, groups=1)
    return pl.pallas_call(
        body,
        grid=(nb, m // tm, n // tn, nk),
        in_specs=in_specs,
        out_specs=pl.BlockSpec((None, tm, tn), lambda b, i, j, k: (b, i, j)),
        out_shape=jax.ShapeDtypeStruct((nb, m, n), out_dtype),
        scratch_shapes=[pltpu.VMEM((tm, tn), F32)],
        compiler_params=_params("parallel", "parallel", "parallel", "arbitrary"),
        name="adaln",
    )(*args)


def _modulated(x, g_ref, sh_ref, sc_ref):
    y = x * lax.rsqrt(jnp.mean(x * x, axis=-1, keepdims=True) + EPS) * g_ref[...]
    return y * (1.0 + sc_ref[0]) + sh_ref[0]


def _modulate_body(x_ref, g_ref, sh_ref, sc_ref, o_ref):
    o_ref[...] = _modulated(x_ref[...], g_ref, sh_ref, sc_ref).astype(o_ref.dtype)


def _mod_specs(rows, d, col):
    return [
        pl.BlockSpec((rows, d), lambda i: (i, 0)),
        pl.BlockSpec((1, d), lambda i: (0, 0)),
        pl.BlockSpec((1, 1, d), lambda i: (i, 0, col)),
        pl.BlockSpec((1, 1, d), lambda i: (i, 0, col + 1)),
    ]


def _modulate(x, g, modrows, col):
    t, d = x.shape
    rows = t // modrows.shape[0]
    return pl.pallas_call(
        _modulate_body,
        grid=(t // rows,),
        in_specs=_mod_specs(rows, d, col),
        out_specs=pl.BlockSpec((rows, d), lambda i: (i, 0)),
        out_shape=jax.ShapeDtypeStruct((t, d), BF16),
        compiler_params=_params("parallel"),
        name="modulate",
    )(x, g.reshape(1, d), modrows, modrows)


def _pack_bf16_pairs(x):
    n = x.shape[1] // 2
    lo = pltpu.bitcast(x[:, :n].astype(BF16).astype(F32), jnp.uint32)
    hi = pltpu.bitcast(x[:, n:].astype(BF16).astype(F32), jnp.uint32)
    return (hi & jnp.uint32(0xFFFF0000)) | (lo >> 16)


def _unpack_bf16_pairs(u):
    return pltpu.bitcast(u << 16, F32), pltpu.bitcast(u & jnp.uint32(0xFFFF0000), F32)


def _modulate_router_body(x_ref, g_ref, sh_ref, sc_ref, rw_ref, rb_ref, h_ref, ti_ref, tg_ref):
    h = _modulated(x_ref[...], g_ref, sh_ref, sc_ref)
    h_ref[...] = _pack_bf16_pairs(h)
    logits = jnp.dot(h, rw_ref[...], precision=lax.Precision.HIGHEST, preferred_element_type=F32) + rb_ref[...]
    lane = lax.broadcasted_iota(jnp.int32, logits.shape, 1).astype(F32)
    n_lane = float(logits.shape[1])
    vals, idxs = [], []
    for _ in range(TOP_K):
        m = jnp.max(logits, axis=-1, keepdims=True)
        idx = jnp.min(jnp.where(logits == m, lane, n_lane), axis=-1, keepdims=True)
        vals.append(m)
        idxs.append(idx)
        logits = jnp.where(lane == idx, -jnp.inf, logits)
    es = [jnp.exp(v - vals[0]) for v in vals]
    inv = 1.0 / functools.reduce(lambda a, b: a + b, es)
    ti = jnp.zeros(lane.shape, jnp.int32)
    tg = jnp.zeros(lane.shape, F32)
    for k in range(TOP_K):
        ti = jnp.where(lane == k, idxs[k].astype(jnp.int32), ti)
        tg = jnp.where(lane == k, es[k] * inv, tg)
    ti_ref[...] = ti
    tg_ref[...] = tg


def _modulate_router(x, g, modrows, col, rw, rb):
    t, d = x.shape
    rows = t // modrows.shape[0]
    e = rw.shape[1]
    rw_p = jnp.zeros((d, V7X_LANES), F32).at[:, :e].set(rw)
    rb_p = jnp.full((1, V7X_LANES), NEG_INF, F32).at[0, :e].set(rb)
    return pl.pallas_call(
        _modulate_router_body,
        grid=(t // rows,),
        in_specs=_mod_specs(rows, d, col) + [
            pl.BlockSpec((d, V7X_LANES), lambda i: (0, 0)),
            pl.BlockSpec((1, V7X_LANES), lambda i: (0, 0)),
        ],
        out_specs=[
            pl.BlockSpec((rows, d // 2), lambda i: (i, 0)),
            pl.BlockSpec((rows, V7X_LANES), lambda i: (i, 0)),
            pl.BlockSpec((rows, V7X_LANES), lambda i: (i, 0)),
        ],
        out_shape=[
            jax.ShapeDtypeStruct((t, d // 2), jnp.uint32),
            jax.ShapeDtypeStruct((t, V7X_LANES), jnp.int32),
            jax.ShapeDtypeStruct((t, V7X_LANES), F32),
        ],
        compiler_params=_params("parallel"),
        name="modulate_router",
    )(x, g.reshape(1, d), modrows, modrows, rw_p, rb_p)


def _dft_tables(n):
    j = jnp.arange(n, dtype=jnp.int32)
    ang = ((j[:, None] * j[None, :]) % n).astype(F32) * (2.0 * math.pi / n)
    s = 1.0 / math.sqrt(n)
    return jnp.cos(ang) * s, jnp.sin(ang) * s


def _fnet_channel_body(x_ref, w_ref, a_ref, b_ref):
    c = a_ref.shape[1]
    r = jnp.dot(x_ref[...], w_ref[...], preferred_element_type=F32)
    a_ref[...] = r[:, :c].astype(a_ref.dtype)
    b_ref[...] = r[:, c:].astype(b_ref.dtype)


def _fnet_channel_dft(h, tm=1024):
    t, d = h.shape
    c = d // FNET_GROUPS
    cos_c, sin_c = _dft_tables(c)
    cs = jnp.concatenate([cos_c, sin_c], axis=1).astype(BF16)
    tm = _pick(t, tm, 8)
    spec = pl.BlockSpec((tm, c), lambda i, g: (i, g))
    return pl.pallas_call(
        _fnet_channel_body,
        grid=(t // tm, FNET_GROUPS),
        in_specs=[spec, pl.BlockSpec((c, 2 * c), lambda i, g: (0, 0))],
        out_specs=[spec, spec],
        out_shape=[jax.ShapeDtypeStruct((t, d), BF16)] * 2,
        compiler_params=_params("parallel", "parallel"),
        name="fnet_channel_dft",
    )(h, cs)


def _fnet_seq_body(c_ref, s_ref, a_ref, b_ref, o_ref, acc_ref, *, nk):
    k = pl.program_id(3)

    @pl.when(k == 0)
    def _():
        acc_ref[...] = jnp.zeros_like(acc_ref)

    acc_ref[...] += (jnp.dot(c_ref[...], a_ref[...], preferred_element_type=F32)
                     + jnp.dot(s_ref[...], b_ref[...], preferred_element_type=F32))

    @pl.when(k == nk - 1)
    def _():
        o_ref[...] = acc_ref[...].astype(o_ref.dtype)


def _fnet_seq_dft(a, b, row_off, nb, seq, tm=1024, tn=1024, tk=1024):
    d = a.shape[1]
    cos_l, sin_l = _dft_tables(seq)
    cos_l, nsin_l = cos_l.astype(BF16), (-sin_l).astype(BF16)
    tm, tn, tk = _pick(seq, tm, 8), _pick(d, tn), _pick(seq, tk)
    nk, mt = seq // tk, seq // tm
    ro = row_off // tk
    tab = pl.BlockSpec((tm, tk), lambda bb, i, j, k: (i, k))
    src = pl.BlockSpec((tk, tn), lambda bb, i, j, k: (ro + bb * nk + k, j))
    return pl.pallas_call(
        functools.partial(_fnet_seq_body, nk=nk),
        grid=(nb, mt, d // tn, nk),
        in_specs=[tab, tab, src, src],
        out_specs=pl.BlockSpec((tm, tn), lambda bb, i, j, k: (bb * mt + i, j)),
        out_shape=jax.ShapeDtypeStruct((nb * seq, d), BF16),
        scratch_shapes=[pltpu.VMEM((tm, tn), F32)],
        compiler_params=_params("parallel", "parallel", "parallel", "arbitrary"),
        name="fnet_seq_dft",
    )(cos_l, nsin_l, a, b)


def _rope_tables(seq, rows_identity):
    half = HEAD_DIM // 2
    inv_freq = ROPE_THETA ** (-jnp.arange(0, half, 2, dtype=F32) / half)
    rows = seq // GRID_W
    row_pos = jnp.repeat(jnp.arange(rows, dtype=jnp.int32), GRID_W).astype(F32)
    col_pos = jnp.tile(jnp.arange(GRID_W, dtype=jnp.int32), rows).astype(F32)
    ar = row_pos[:, None] * inv_freq
    ac = col_pos[:, None] * inv_freq
    cos_t = jnp.concatenate([jnp.cos(ar), jnp.cos(ar), jnp.cos(ac), jnp.cos(ac)], axis=-1)
    sin_t = jnp.concatenate([-jnp.sin(ar), jnp.sin(ar), -jnp.sin(ac), jnp.sin(ac)], axis=-1)
    cos_t = jnp.concatenate([cos_t, jnp.ones((rows_identity, HEAD_DIM), F32)], axis=0)
    sin_t = jnp.concatenate([sin_t, jnp.zeros((rows_identity, HEAD_DIM), F32)], axis=0)
    return cos_t, sin_t


def _qknorm_rope_body(x_ref, g_ref, cos_ref, sin_ref, o_ref):
    cos_t, sin_t, gain = cos_ref[...], sin_ref[...], g_ref[...]
    lane = lax.broadcasted_iota(jnp.int32, cos_t.shape, 1)
    first = (lane % (HEAD_DIM // 2)) < (HEAD_DIM // 4)
    for h in range(x_ref.shape[1] // HEAD_DIM):
        sl = slice(h * HEAD_DIM, (h + 1) * HEAD_DIM)
        x = x_ref[:, sl]
        y = x * lax.rsqrt(jnp.mean(x * x, axis=-1, keepdims=True) + EPS) * gain
        partner = jnp.where(first, pltpu.roll(y, HEAD_DIM - HEAD_DIM // 4, 1), pltpu.roll(y, HEAD_DIM // 4, 1))
        o_ref[:, sl] = (y * cos_t + partner * sin_t).astype(o_ref.dtype)


def _qknorm_rope(qkv, gain, cos_t, sin_t, col_off, n_cols, rows, tab_index, out_dtype, tc=1024):
    t = qkv.shape[0]
    tc = _pick(n_cols, tc)
    off = col_off // tc
    return pl.pallas_call(
        _qknorm_rope_body,
        grid=(t // rows, n_cols // tc),
        in_specs=[
            pl.BlockSpec((rows, tc), lambda i, j: (i, off + j)),
            pl.BlockSpec((1, HEAD_DIM), lambda i, j: (0, 0)),
            pl.BlockSpec((rows, HEAD_DIM), lambda i, j: (tab_index(i), 0)),
            pl.BlockSpec((rows, HEAD_DIM), lambda i, j: (tab_index(i), 0)),
        ],
        out_specs=pl.BlockSpec((rows, tc), lambda i, j: (i, j)),
        out_shape=jax.ShapeDtypeStruct((t, n_cols), out_dtype),
        compiler_params=_params("parallel", "parallel"),
        name="qknorm_rope",
    )(qkv, gain.reshape(1, HEAD_DIM), cos_t, sin_t)


def _attn_body(sink_ref, q_ref, *refs, n_win):
    kw, vw = refs[:n_win], refs[n_win:2 * n_win]
    rest = refs[2 * n_win:]
    bias_ref = rest[0] if n_win else None
    kc_ref, vc_ref, o_ref = rest[-3:]
    g = pl.program_id(1)
    rep = q_ref.shape[1] // HEAD_DIM
    nq = q_ref.shape[0]
    scale = HEAD_DIM ** -0.5
    nt = (((1,), (1,)), ((), ()))
    q = jnp.concatenate([q_ref[:, r * HEAD_DIM:(r + 1) * HEAD_DIM] for r in range(rep)], axis=0).astype(BF16)
    sink = jnp.concatenate([jnp.full((nq, 1), sink_ref[g * rep + r], F32) for r in range(rep)], axis=0)
    s_c = lax.dot_general(q, kc_ref[...].astype(BF16), nt, preferred_element_type=F32) * scale
    m = jnp.maximum(jnp.max(s_c, axis=-1, keepdims=True), sink)
    if n_win:
        keys = jnp.concatenate([r[...].astype(BF16) for r in kw], axis=0)
        vals = jnp.concatenate([r[...].astype(BF16) for r in vw], axis=0)
        s_w = lax.dot_general(q, keys, nt, preferred_element_type=F32) * scale
        nk = s_w.shape[1]
        s_w = (s_w.reshape(rep, nq, nk) + bias_ref[...]).reshape(rep * nq, nk)
        m = jnp.maximum(m, jnp.max(s_w, axis=-1, keepdims=True))
    p_c = jnp.exp(s_c - m)
    denom = jnp.sum(p_c, axis=-1, keepdims=True) + jnp.exp(sink - m)
    if n_win:
        p_w = jnp.exp(s_w - m)
        denom = denom + jnp.sum(p_w, axis=-1, keepdims=True)
    inv = 1.0 / denom
    o = jnp.dot((p_c * inv).astype(BF16), vc_ref[...].astype(BF16), preferred_element_type=F32)
    if n_win:
        o = o + jnp.dot((p_w * inv).astype(BF16), vals, preferred_element_type=F32)
    for r in range(rep):
        o_ref[:, r * HEAD_DIM:(r + 1) * HEAD_DIM] = o[r * nq:(r + 1) * nq].astype(o_ref.dtype)


def _window_bias(seq):
    nblk = seq // BLOCK
    qi = jnp.arange(BLOCK, dtype=jnp.int32)[:, None]
    kj = jnp.arange(3 * BLOCK, dtype=jnp.int32)[None, :]

    def one(i):
        kpos = i * BLOCK - BLOCK + kj
        band = (jnp.abs(i * BLOCK + qi - kpos) <= WINDOW) & (kpos >= 0) & (kpos < seq)
        return jnp.where(band, 0.0, NEG_INF).astype(F32)

    return jnp.stack([one(0), one(min(1, nblk - 1)), one(nblk - 1)])


def _attention(qn, kn, qkv, v_col_off, sink, kc, vc, row_off, nb, seq, windowed):
    rep = N_HEADS // N_KV
    qw = rep * HEAD_DIM
    vo = v_col_off // HEAD_DIM
    if windowed:
        nblk = seq // BLOCK
        ro = row_off // BLOCK
        lc = kc.shape[1]

        def win(shift, col):
            return pl.BlockSpec(
                (BLOCK, HEAD_DIM),
                lambda b, g, i, s: (ro + b * nblk + jnp.clip(i + shift, 0, nblk - 1), col + g))

        in_specs = [pl.BlockSpec((BLOCK, qw), lambda b, g, i, s: (ro + b * nblk + i, g))]
        in_specs += [win(-1, 0), win(0, 0), win(1, 0), win(-1, vo), win(0, vo), win(1, vo)]
        in_specs.append(pl.BlockSpec(
            (None, BLOCK, 3 * BLOCK),
            lambda b, g, i, s: (jnp.where(i == 0, 0, jnp.where(i == nblk - 1, 2, 1)), 0, 0)))
        in_specs += [pl.BlockSpec((None, lc, HEAD_DIM), lambda b, g, i, s: (b, 0, g))] * 2
        args = [qn, kn, kn, kn, qkv, qkv, qkv, _window_bias(seq), kc, vc]
        grid = (nb, N_KV, nblk)
        out_spec = pl.BlockSpec((BLOCK, qw), lambda b, g, i, s: (b * nblk + i, g))
        body = functools.partial(_attn_body, n_win=3)
    else:
        ro = row_off // seq
        in_specs = [
            pl.BlockSpec((seq, qw), lambda b, g, i, s: (ro + b, g)),
            pl.BlockSpec((seq, HEAD_DIM), lambda b, g, i, s: (ro + b, g)),
            pl.BlockSpec((seq, HEAD_DIM), lambda b, g, i, s: (ro + b, vo + g)),
        ]
        args = [qn, kn, qkv]
        grid = (nb, N_KV, 1)
        out_spec = pl.BlockSpec((seq, qw), lambda b, g, i, s: (b, g))
        body = functools.partial(_attn_body, n_win=0)
    return pl.pallas_call(
        body,
        grid_spec=pltpu.PrefetchScalarGridSpec(
            num_scalar_prefetch=1, grid=grid, in_specs=in_specs, out_specs=out_spec),
        out_shape=jax.ShapeDtypeStruct((nb * seq, N_HEADS * HEAD_DIM), BF16),
        compiler_params=_params("parallel", "parallel", "parallel"),
        name="attention_window" if windowed else "attention_context",
    )(sink.astype(F32), *args)


def _conv_silu_body(x_ref, w_ref, b_ref, o_ref):
    x = x_ref[...]
    seq = x.shape[0]
    pad = SSD_CONV // 2
    t = lax.broadcasted_iota(jnp.int32, (seq, 1), 0)
    acc = b_ref[...] + x * w_ref[pad:pad + 1, :]
    for k in range(SSD_CONV):
        s = k - pad
        if s == 0:
            continue
        shifted = pltpu.roll(x, (-s) % seq, 0)
        acc = acc + jnp.where((t + s >= 0) & (t + s < seq), shifted, 0.0) * w_ref[k:k + 1, :]
    o_ref[...] = acc * jax.nn.sigmoid(acc)


def _conv_silu(zx, col_off, n_cols, w, b, row_off, nb, seq, tc=256):
    tc = _pick(n_cols, tc)
    co, ro = col_off // tc, row_off // seq
    return pl.pallas_call(
        _conv_silu_body,
        grid=(nb, n_cols // tc),
        in_specs=[
            pl.BlockSpec((seq, tc), lambda b, j: (ro + b, co + j)),
            pl.BlockSpec((SSD_CONV, tc), lambda b, j: (0, j)),
            pl.BlockSpec((1, tc), lambda b, j: (0, j)),
        ],
        out_specs=pl.BlockSpec((seq, tc), lambda b, j: (b, j)),
        out_shape=jax.ShapeDtypeStruct((nb * seq, n_cols), F32),
        compiler_params=_params("parallel", "parallel"),
        name="ssd_conv_silu",
    )(zx, w, b.reshape(1, n_cols))


def _softplus(x):
    return jnp.maximum(x, 0.0) + jnp.log(1.0 + jnp.exp(-jnp.abs(x)))


def _ssd_scan_body(x_ref, b_ref, c_ref, dt_ref, dtt_ref, db_ref, dbt_ref, al_ref, alt_ref, *refs,
                   nc, reverse, has_h0):
    if has_h0:
        h0_ref, y_ref, hf_ref, h_ref = refs
    else:
        y_ref, hf_ref, h_ref = refs
    c_idx = pl.program_id(2)
    q = x_ref.shape[0]
    hpg = dt_ref.shape[1]
    p = x_ref.shape[1] // hpg

    @pl.when(c_idx == 0)
    def _():
        h_ref[...] = h0_ref[...] if has_h0 else jnp.zeros_like(h_ref)

    hi = lax.Precision.HIGHEST
    ii = lax.broadcasted_iota(jnp.int32, (q, q), 0)
    jj = lax.broadcasted_iota(jnp.int32, (q, q), 1)
    incl = (jj >= ii) if reverse else (jj <= ii)
    inclf = incl.astype(F32)
    dt = _softplus(dt_ref[...] + db_ref[...])
    dtt = _softplus(dtt_ref[...] + dbt_ref[...])
    da = dt * (-jnp.exp(al_ref[...]))
    dat = dtt * (-jnp.exp(alt_ref[...]))
    acum = jnp.dot(inclf, da, precision=hi, preferred_element_type=F32)
    acumt = lax.dot_general(dat, inclf, (((1,), (1,)), ((), ())), precision=hi,
                            preferred_element_type=F32)
    total = jnp.sum(da, axis=0, keepdims=True)
    total_t = jnp.sum(dat, axis=1, keepdims=True)
    xb = x_ref[...]
    bm = b_ref[...].astype(BF16)
    cm = c_ref[...].astype(BF16)
    cb = lax.dot_general(cm, bm, (((1,), (1,)), ((), ())), preferred_element_type=F32)
    h = h_ref[...]
    ch = lax.dot_general(cm, h.astype(BF16), (((1,), (1,)), ((), ())), preferred_element_type=F32)
    lane = lax.broadcasted_iota(jnp.int32, (q, 2 * p), 1)
    decay = []
    for r in range(0, hpg, 2):
        sl = slice(r * p, (r + 2) * p)
        mats, ex = [], []
        for rr in (r, r + 1):
            ai = jnp.broadcast_to(acum[:, rr:rr + 1], (q, q))
            lmat = jnp.exp(jnp.where(incl, ai - acumt[rr:rr + 1, :], -jnp.inf))
            mats.append((cb * lmat * dtt[rr:rr + 1, :]).astype(BF16))
            ex.append(jnp.exp(ai[:, :2 * p]))
            decay.append(jnp.broadcast_to(jnp.exp(total[:, rr:rr + 1]), (p, 1)))
        x2 = xb[:, sl]
        rhs = jnp.concatenate([jnp.where(lane < p, x2, 0.0), jnp.where(lane >= p, x2, 0.0)], axis=0).astype(BF16)
        intra = jnp.dot(jnp.concatenate(mats, axis=1), rhs, preferred_element_type=F32)
        y_ref[:, sl] = intra + ch[:, sl] * jnp.where(lane < p, ex[0], ex[1])
    w_t = jnp.exp(total_t - acumt) * dtt
    xwt = (xb.T.reshape(hpg, p, q) * w_t[:, None, :]).reshape(hpg * p, q).astype(BF16)
    h_new = h * jnp.concatenate(decay, axis=0) + jnp.dot(xwt, bm, preferred_element_type=F32)
    h_ref[...] = h_new

    @pl.when(c_idx == nc - 1)
    def _():
        hf_ref[...] = h_new


def _ssd_scan(xbc, dt_r, dt_t, dt_bias, a_log, h0, direction, nb, seq, inner):
    nc = seq // SSD_CHUNK
    hpg = inner // SSD_HEADDIM // SSD_GROUPS
    gw = hpg * SSD_HEADDIM
    reverse = direction == 1
    bo = inner // SSD_STATE
    co = bo + SSD_GROUPS

    def tok(b, c):
        return b * nc + (nc - 1 - c if reverse else c)

    db = dt_bias.reshape(2, SSD_GROUPS, 1, hpg)
    al = a_log.reshape(2, SSD_GROUPS, 1, hpg)
    in_specs = [
        pl.BlockSpec((SSD_CHUNK, gw), lambda b, g, c: (tok(b, c), g)),
        pl.BlockSpec((SSD_CHUNK, SSD_STATE), lambda b, g, c: (tok(b, c), bo + g)),
        pl.BlockSpec((SSD_CHUNK, SSD_STATE), lambda b, g, c: (tok(b, c), co + g)),
        pl.BlockSpec((None, None, SSD_CHUNK, hpg), lambda b, g, c: (direction, g, tok(b, c), 0)),
        pl.BlockSpec((None, None, hpg, SSD_CHUNK), lambda b, g, c: (direction, g, 0, tok(b, c))),
        pl.BlockSpec((None, None, 1, hpg), lambda b, g, c: (direction, g, 0, 0)),
        pl.BlockSpec((None, None, hpg, 1), lambda b, g, c: (direction, g, 0, 0)),
        pl.BlockSpec((None, None, 1, hpg), lambda b, g, c: (direction, g, 0, 0)),
        pl.BlockSpec((None, None, hpg, 1), lambda b, g, c: (direction, g, 0, 0)),
    ]
    args = [xbc, xbc, xbc, dt_r, dt_t, db, db.reshape(2, SSD_GROUPS, hpg, 1), al, al.reshape(2, SSD_GROUPS, hpg, 1)]
    if h0 is not None:
        in_specs.append(pl.BlockSpec((None, None, gw, SSD_STATE), lambda b, g, c: (b, g, 0, 0)))
        args.append(h0)
    body = functools.partial(_ssd_scan_body, nc=nc, reverse=reverse, has_h0=h0 is not None)
    return pl.pallas_call(
        body,
        grid=(nb, SSD_GROUPS, nc),
        in_specs=in_specs,
        out_specs=[
            pl.BlockSpec((SSD_CHUNK, gw), lambda b, g, c: (tok(b, c), g)),
            pl.BlockSpec((None, None, gw, SSD_STATE), lambda b, g, c: (b, g, 0, 0)),
        ],
        out_shape=[
            jax.ShapeDtypeStruct((nb * seq, inner), F32),
            jax.ShapeDtypeStruct((nb, SSD_GROUPS, gw, SSD_STATE), F32),
        ],
        scratch_shapes=[pltpu.VMEM((gw, SSD_STATE), F32)],
        compiler_params=_params("parallel", "parallel", "arbitrary"),
        name="ssd_scan_bwd" if reverse else "ssd_scan_fwd",
    )(*args)


def _gated_norm_body(yf_ref, yb_ref, xs_ref, z_ref, d_ref, g_ref, o_ref):
    y = yf_ref[...] + yb_ref[...] + d_ref[...] * xs_ref[...]
    z = z_ref[...]
    u = y * (z * jax.nn.sigmoid(z))
    u = u * lax.rsqrt(jnp.mean(u * u, axis=-1, keepdims=True) + EPS)
    o_ref[...] = (u * g_ref[...]).astype(o_ref.dtype)


def _gated_norm(yf, yb, xbc, zx, z_row_off, d_cols, g, inner, tr=256):
    t = yf.shape[0]
    gw = inner // SSD_GROUPS
    tr = _pick(t, tr, 8)
    zo = z_row_off // tr
    spec = pl.BlockSpec((tr, gw), lambda i, j: (i, j))
    vec = pl.BlockSpec((1, gw), lambda i, j: (0, j))
    return pl.pallas_call(
        _gated_norm_body,
        grid=(t // tr, SSD_GROUPS),
        in_specs=[spec, spec, spec, pl.BlockSpec((tr, gw), lambda i, j: (zo + i, j)), vec, vec],
        out_specs=spec,
        out_shape=jax.ShapeDtypeStruct((t, inner), BF16),
        compiler_params=_params("parallel", "parallel"),
        name="ssd_gated_norm",
    )(yf, yb, xbc, zx, d_cols.reshape(1, inner), g.reshape(1, inner))


def _moe_route(ti, n_experts, bm):
    t = ti.shape[0]
    n_slots = t * TOP_K
    flat_e = ti.reshape(-1)
    onehot = (flat_e[:, None] == jnp.arange(n_experts, dtype=jnp.int32)[None, :]).astype(jnp.int32)
    csum = jnp.cumsum(onehot, axis=0)
    counts = csum[-1]
    rank = jnp.take_along_axis(csum, flat_e[:, None], axis=1)[:, 0] - 1
    padded = (counts + bm - 1) // bm * bm
    pad_end = jnp.cumsum(padded)
    dest = (pad_end - padded)[flat_e] + rank
    n_blocks = n_slots // bm + n_experts
    starts = jnp.arange(n_blocks, dtype=jnp.int32) * bm
    n_used = pad_end[-1] // bm
    block_exp = jnp.minimum(jnp.sum(starts[:, None] >= pad_end[None, :], axis=1), n_experts - 1).astype(jnp.int32)
    block_exp = jnp.where(jnp.arange(n_blocks) < n_used, block_exp, block_exp[jnp.maximum(n_used - 1, 0)])
    return n_blocks * bm, block_exp, n_used.astype(jnp.int32).reshape(1), dest.reshape(t, TOP_K)


def _row_copy(src_hbm, row, dst, sem):
    return pltpu.make_async_copy(src_hbm.at[pl.ds(row, 1)], dst, sem)


def _burst_rows(n, rows):
    per_sub = rows // V7X_SUBLANES // DMA_UNROLL
    base = (n % per_sub) * (DMA_UNROLL * V7X_SUBLANES) + n // per_sub
    return [base + u * V7X_SUBLANES for u in range(DMA_UNROLL)]


def _dispatch_rows_body(dest_ref, hp_ref, init_hbm, out_hbm, sem):
    del init_hbm
    tr = hp_ref.shape[0]

    def copy(r, slot):
        return pltpu.make_async_copy(hp_ref.at[pl.ds(r, 1)], out_hbm.at[pl.ds(slot, 1)], sem)

    def burst(b, carry):
        for r in _burst_rows(b, tr):
            for k in range(TOP_K):
                copy(r, dest_ref[0, 0, r * TOP_K + k]).start()
        return carry

    def wait(r, carry):
        for k in range(TOP_K):
            copy(r, 0).wait()
        return carry

    lax.fori_loop(0, tr // DMA_UNROLL, burst, 0)
    lax.fori_loop(0, tr, wait, 0, unroll=DMA_UNROLL // TOP_K)


def _dispatch_rows(hp, dest, cap, tr=256):
    t, half = hp.shape
    tr = _pick(t, tr, 8)
    assert tr % (V7X_SUBLANES * DMA_UNROLL) == 0
    return pl.pallas_call(
        _dispatch_rows_body,
        grid=(t // tr,),
        in_specs=[
            pl.BlockSpec((1, 1, tr * TOP_K), lambda i: (i, 0, 0), memory_space=pltpu.SMEM),
            pl.BlockSpec((tr, half), lambda i: (i, 0)),
            pl.BlockSpec(memory_space=pl.ANY),
        ],
        out_specs=pl.BlockSpec(memory_space=pl.ANY),
        out_shape=jax.ShapeDtypeStruct((cap, half), jnp.uint32),
        scratch_shapes=[pltpu.SemaphoreType.DMA(())],
        input_output_aliases={2: 0},
        compiler_params=_params("arbitrary"),
        name="moe_dispatch",
    )(dest.reshape(t // tr, 1, tr * TOP_K), hp, jnp.zeros((cap, half), jnp.uint32))


def _expert_changed(be_ref, m):
    return (m == 0) | (be_ref[m] != be_ref[jnp.maximum(m - 1, 0)])


def _moe_up_body(be_ref, nu_ref, x_ref, wg_ref, wu_ref, bg_ref, bu_ref, o_ref, wg_s, wu_s):
    m = pl.program_id(1)

    @pl.when(m < nu_ref[0])
    def _():
        @pl.when(_expert_changed(be_ref, m))
        def _():
            wg_s[...] = wg_ref[...].astype(BF16)
            wu_s[...] = wu_ref[...].astype(BF16)

        half = x_ref.shape[1]
        lo, hi = _unpack_bf16_pairs(x_ref[...])
        lo, hi = lo.astype(BF16), hi.astype(BF16)
        g = (jnp.dot(lo, wg_s[:half, :], preferred_element_type=F32)
             + jnp.dot(hi, wg_s[half:, :], preferred_element_type=F32) + bg_ref[...])
        u = (jnp.dot(lo, wu_s[:half, :], preferred_element_type=F32)
             + jnp.dot(hi, wu_s[half:, :], preferred_element_type=F32) + bu_ref[...])
        g = jnp.minimum(g, SWIGLU_LIMIT)
        u = jnp.clip(u, -SWIGLU_LIMIT, SWIGLU_LIMIT)
        o_ref[...] = ((u + 1.0) * (g * jax.nn.sigmoid(SWIGLU_ALPHA * g))).astype(o_ref.dtype)

    @pl.when(m >= nu_ref[0])
    def _():
        o_ref[...] = jnp.zeros_like(o_ref)


def _moe_up(xs, w, b, layer, block_exp, n_used, bm, tn=512):
    cap, half = xs.shape
    d = 2 * half
    e, f2 = b.shape[1], b.shape[2]
    f = f2 // 2
    tn = _pick(f, tn)
    nf = f // tn
    b4 = b.reshape(b.shape[0], e, 1, f2)
    wspec = lambda off: pl.BlockSpec((None, None, d, tn), lambda j, m, be, nu: (layer, be[m], 0, off + j))
    bspec = lambda off: pl.BlockSpec((None, None, 1, tn), lambda j, m, be, nu: (layer, be[m], 0, off + j))
    return pl.pallas_call(
        _moe_up_body,
        grid_spec=pltpu.PrefetchScalarGridSpec(
            num_scalar_prefetch=2,
            grid=(nf, cap // bm),
            in_specs=[pl.BlockSpec((bm, half), lambda j, m, be, nu: (m, 0)), wspec(0), wspec(nf), bspec(0), bspec(nf)],
            out_specs=pl.BlockSpec((bm, tn), lambda j, m, be, nu: (m, j)),
            scratch_shapes=[pltpu.VMEM((d, tn), BF16), pltpu.VMEM((d, tn), BF16)],
        ),
        out_shape=jax.ShapeDtypeStruct((cap, f), BF16),
        compiler_params=_params("parallel", "arbitrary"),
        name="moe_up",
    )(block_exp, n_used, xs, w, w, b4, b4)


def _moe_down_body(be_ref, nu_ref, a_ref, w_ref, b_ref, o_ref, w_s):
    m = pl.program_id(1)

    @pl.when(m < nu_ref[0])
    def _():
        @pl.when(_expert_changed(be_ref, m))
        def _():
            w_s[...] = w_ref[...].astype(BF16)

        o_ref[...] = _pack_bf16_pairs(jnp.dot(a_ref[...], w_s[...], preferred_element_type=F32) + b_ref[...])

    @pl.when(m >= nu_ref[0])
    def _():
        o_ref[...] = jnp.zeros_like(o_ref)


def _moe_down(act, w, b, layer, block_exp, n_used, bm, tn):
    cap, f = act.shape
    e, d = b.shape[1], b.shape[2]
    b4 = b.reshape(b.shape[0], e, 1, d)
    return pl.pallas_call(
        _moe_down_body,
        grid_spec=pltpu.PrefetchScalarGridSpec(
            num_scalar_prefetch=2,
            grid=(d // tn, cap // bm),
            in_specs=[
                pl.BlockSpec((bm, f), lambda j, m, be, nu: (m, 0)),
                pl.BlockSpec((None, None, f, tn), lambda j, m, be, nu: (layer, be[m], 0, j)),
                pl.BlockSpec((None, None, 1, tn), lambda j, m, be, nu: (layer, be[m], 0, j)),
            ],
            out_specs=pl.BlockSpec((bm, tn // 2), lambda j, m, be, nu: (m, j)),
            scratch_shapes=[pltpu.VMEM((f, tn), BF16)],
        ),
        out_shape=jax.ShapeDtypeStruct((cap, d // 2), jnp.uint32),
        compiler_params=_params("parallel", "arbitrary"),
        name="moe_down",
    )(block_exp, n_used, act, w, b4)


def _moe_combine_body(pos_ref, nxt_ref, x_ref, tg_ref, g2_ref, y_hbm, o_ref, buf, sem, *, tn):
    tr = x_ref.shape[0]
    i, n = pl.program_id(0), pl.num_programs(0)
    slot = i % 2

    def start_block(idx_ref, s):
        def burst(b, carry):
            for r in _burst_rows(b, tr):
                for k in range(TOP_K):
                    _row_copy(y_hbm, idx_ref[0, 0, r * TOP_K + k], buf.at[s, k, pl.ds(r, 1)], sem.at[s]).start()
            return carry

        lax.fori_loop(0, tr // DMA_UNROLL, burst, 0)

    @pl.when(i == 0)
    def _():
        start_block(pos_ref, 0)

    @pl.when(i + 1 < n)
    def _():
        start_block(nxt_ref, 1 - slot)

    def wait(r, carry):
        for k in range(TOP_K):
            _row_copy(y_hbm, 0, buf.at[slot, k, pl.ds(r, 1)], sem.at[slot]).wait()
        return carry

    lax.fori_loop(0, tr, wait, 0, unroll=DMA_UNROLL // TOP_K)
    tg = tg_ref[...]
    pw = tn // 2
    for j in range(x_ref.shape[1] // tn):
        lo = hi = None
        for k in range(TOP_K):
            lk, hk = _unpack_bf16_pairs(buf[slot, k, :, j * pw:(j + 1) * pw])
            gk = tg[:, k:k + 1]
            lo = lk * gk if lo is None else lo + lk * gk
            hi = hk * gk if hi is None else hi + hk * gk
        for part, y in ((slice(j * tn, j * tn + pw), lo), (slice(j * tn + pw, (j + 1) * tn), hi)):
            o_ref[:, part] = x_ref[:, part] + g2_ref[0, :, part] * y


def _moe_combine(x, y_slots, pos, tg, modrows, gate_col, tn, tr=128):
    t, d = x.shape
    rows = t // modrows.shape[0]
    tr = _pick(rows, tr, 8)
    per = rows // tr
    nblk = t // tr
    assert tr % (V7X_SUBLANES * DMA_UNROLL) == 0
    pos3 = pos.reshape(nblk, 1, tr * TOP_K)
    return pl.pallas_call(
        functools.partial(_moe_combine_body, tn=tn),
        grid=(nblk,),
        in_specs=[
            pl.BlockSpec((1, 1, tr * TOP_K), lambda i: (i, 0, 0), memory_space=pltpu.SMEM),
            pl.BlockSpec((1, 1, tr * TOP_K), lambda i: (jnp.minimum(i + 1, nblk - 1), 0, 0), memory_space=pltpu.SMEM),
            pl.BlockSpec((tr, d), lambda i: (i, 0)),
            pl.BlockSpec((tr, V7X_LANES), lambda i: (i, 0)),
            pl.BlockSpec((1, 1, d), lambda i: (i // per, 0, gate_col)),
            pl.BlockSpec(memory_space=pl.ANY),
        ],
        out_specs=pl.BlockSpec((tr, d), lambda i: (i, 0)),
        out_shape=jax.ShapeDtypeStruct((t, d), F32),
        scratch_shapes=[pltpu.VMEM((2, TOP_K, tr, d // 2), jnp.uint32), pltpu.SemaphoreType.DMA((2,))],
        compiler_params=_params("arbitrary"),
        name="moe_combine",
    )(pos3, pos3, x, tg, modrows, y_slots)


def _moe_ffn(x, g, modrows, layer, router_w, router_b, w_gate_up, b_gate_up, w_down, b_down):
    n_experts = router_w.shape[-1]
    bm = MOE_BLOCK_ROWS
    tn_down = _pick(x.shape[1], MOE_DOWN_COLS)
    hp, ti, tg = _modulate_router(x, g, modrows, 3, router_w[layer], router_b[layer])
    cap, block_exp, n_used, pos = _moe_route(ti[:, :TOP_K], n_experts, bm)
    xs = _dispatch_rows(hp, pos, cap)
    act = _moe_up(xs, w_gate_up, b_gate_up, layer, block_exp, n_used, bm)
    y_slots = _moe_down(act, w_down, b_down, layer, block_exp, n_used, bm, tn_down)
    return _moe_combine(x, y_slots, pos, tg, modrows, 5, tn_down)


def kernel(x_prompt, x_sample, cache_attn_k, cache_attn_v, state_ssd_fwd, state_ssd_bwd, c, c_ctx, w_ada, b_ada, norm1_g, norm2_g, fnet_w_out, fnet_b_out, attn_w_qkv, attn_q_gain, attn_k_gain, attn_sink, attn_w_out, ssd_w_in, ssd_conv_w, ssd_conv_b, ssd_dt_bias, ssd_a_log, ssd_d, ssd_norm_g, ssd_w_out, moe_router_w, moe_router_b, moe_w_gate_up, moe_b_gate_up, moe_w_down, moe_b_down):
    bp, lp, d = x_prompt.shape
    bs, ls, _ = x_sample.shape
    tp, ts = bp * lp, bs * ls
    t = tp + ts
    depth = w_ada.shape[0]
    rows = math.gcd(lp, ls)
    x = jnp.concatenate([x_prompt.reshape(tp, d), x_sample.reshape(ts, d)], axis=0)

    assert 1 + bs <= COND_ROWS
    cond = jnp.zeros((COND_ROWS, d), F32).at[0].set(c_ctx).at[1:1 + bs].set(c)
    mod = _matmul_wbatched(cond, w_ada, b_ada, silu_in=True, tm=COND_ROWS, tn=1024, tk=2048)
    group_cond = jnp.concatenate([jnp.zeros((tp // rows,), jnp.int32),
                                  1 + jnp.arange(ts // rows, dtype=jnp.int32) // (ls // rows)])

    q_dim, kv_dim = N_HEADS * HEAD_DIM, N_KV * HEAD_DIM
    fnet_w = fnet_w_out.astype(BF16)
    new_k, new_v, new_f, new_b = [], [], [], []
    for l in range(depth):
        kind, j = l % N_MIXERS, l // N_MIXERS
        modrows = mod[l][group_cond].reshape(t // rows, 1, 6 * d)
        h = _modulate(x, norm1_g[l], modrows, 0)
        if kind == 0:
            fa, fb = _fnet_channel_dft(h)
            mixed = jnp.concatenate([_fnet_seq_dft(fa, fb, 0, bp, lp), _fnet_seq_dft(fa, fb, tp, bs, ls)], axis=0)
            x = _matmul(mixed, fnet_w, w_lead=(j,), bias=fnet_b_out[j], res=x, gate=modrows, gate_col=2,
                        name="fnet_out")
        elif kind == 1:
            qkv = _matmul(h, attn_w_qkv.astype(BF16), w_lead=(j,), tm=2048, name="attn_qkv")
            cos_t, sin_t = _rope_tables(ls, rows)
            n_tab = ls // rows
            tab = lambda i: jnp.where(i < tp // rows, n_tab, (i - tp // rows) % n_tab)
            qn = _qknorm_rope(qkv, attn_q_gain[j], cos_t, sin_t, 0, q_dim, rows, tab, BF16)
            kn = _qknorm_rope(qkv, attn_k_gain[j], cos_t, sin_t, q_dim, kv_dim, rows, tab, F32)
            new_k.append(kn[:tp].reshape(bp, lp, N_KV, HEAD_DIM))
            new_v.append(qkv[:tp, q_dim + kv_dim:].reshape(bp, lp, N_KV, HEAD_DIM))
            op = _attention(qn, kn, qkv, q_dim + kv_dim, attn_sink[j], None, None, 0, bp, lp, False)
            kc = cache_attn_k[:, j].reshape(bs, -1, kv_dim)
            vc = cache_attn_v[:, j].reshape(bs, -1, kv_dim)
            os_ = _attention(qn, kn, qkv, q_dim + kv_dim, attn_sink[j], kc, vc, tp, bs, ls, True)
            o = jnp.concatenate([op, os_], axis=0)
            x = _matmul(o, attn_w_out.astype(BF16), w_lead=(j,), res=x, gate=modrows, gate_col=2, name="attn_out")
        else:
            inner = ssd_w_out.shape[1]
            heads = inner // SSD_HEADDIM
            hpg = heads // SSD_GROUPS
            gn = SSD_GROUPS * SSD_STATE
            n_zx = 2 * inner + 2 * gn
            tn_zx = _pick(n_zx, 1024)
            w_in = ssd_w_in.astype(BF16)
            zx = _matmul(h, w_in, w_lead=(j,), n_out=n_zx, tm=2048, tn=tn_zx, name="ssd_in")
            dt_raw = _matmul(h, w_in, w_lead=(j,), n_off=n_zx // (2 * heads), n_out=2 * heads, tn=2 * heads,
                             name="ssd_in_dt")
            dt_r = dt_raw.reshape(t, 2, SSD_GROUPS, hpg).transpose(1, 2, 0, 3)
            dt_t = dt_raw.reshape(t, 2, SSD_GROUPS, hpg).transpose(1, 2, 3, 0)
            d_cols = jnp.repeat(ssd_d[j], SSD_HEADDIM)
            ys = []
            for (row_off, nb, seq, s_f, s_b) in ((0, bp, lp, None, None),
                                                 (tp, bs, ls, state_ssd_fwd[:, j], state_ssd_bwd[:, j])):
                xbc = _conv_silu(zx, inner, inner + 2 * gn, ssd_conv_w[j], ssd_conv_b[j], row_off, nb, seq)
                sl = slice(row_off, row_off + nb * seq)
                h0 = [None if s is None else s.reshape(nb, SSD_GROUPS, hpg * SSD_HEADDIM, SSD_STATE) for s in (s_f, s_b)]
                yf, hf = _ssd_scan(xbc, dt_r[:, :, sl], dt_t[:, :, :, sl], ssd_dt_bias[j], ssd_a_log[j], h0[0], 0, nb, seq, inner)
                yb, hb = _ssd_scan(xbc, dt_r[:, :, sl], dt_t[:, :, :, sl], ssd_dt_bias[j], ssd_a_log[j], h0[1], 1, nb, seq, inner)
                if s_f is None:
                    new_f.append(hf.reshape(nb, heads, SSD_HEADDIM, SSD_STATE))
                    new_b.append(hb.reshape(nb, heads, SSD_HEADDIM, SSD_STATE))
                ys.append(_gated_norm(yf, yb, xbc, zx, row_off, d_cols, ssd_norm_g[j], inner))
            x = _matmul(jnp.concatenate(ys, axis=0), ssd_w_out.astype(BF16), w_lead=(j,), res=x, gate=modrows,
                        gate_col=2, name="ssd_out")
        x = _moe_ffn(x, norm2_g[l], modrows, l, moe_router_w, moe_router_b,
                     moe_w_gate_up, moe_b_gate_up, moe_w_down, moe_b_down)

    return (x[:tp].reshape(bp, lp, d), x[tp:].reshape(bs, ls, d),
            jnp.stack(new_k, axis=1), jnp.stack(new_v, axis=1),
            jnp.stack(new_f, axis=1), jnp.stack(new_b, axis=1))
```

```python
import functools
import math

import jax
import jax.numpy as jnp
from jax import lax
from jax.experimental import pallas as pl
from jax.experimental.pallas import tpu as pltpu

GRID_W = 64
N_MIXERS = 3
EPS = 1e-6
FNET_GROUPS = 8
N_HEADS = 32
N_KV = 8
HEAD_DIM = 128
WINDOW = 128
BLOCK = 128
ROPE_THETA = 10000.0
NEG_INF = -1e30
SSD_HEADDIM = 64
SSD_GROUPS = 8
SSD_STATE = 128
SSD_CONV = 5
SSD_CHUNK = 128
TOP_K = 4
SWIGLU_LIMIT = 7.0
SWIGLU_ALPHA = 1.702

V7X_LANES = 128
V7X_SUBLANES = 8
V7X_VMEM_LIMIT_BYTES = 56 * 1024 * 1024
COND_ROWS = 16
MOE_BLOCK_ROWS = 512
MOE_DOWN_COLS = 2048
DMA_UNROLL = 8

F32 = jnp.float32
BF16 = jnp.bfloat16


def _params(*sem):
    return pltpu.CompilerParams(dimension_semantics=sem, vmem_limit_bytes=V7X_VMEM_LIMIT_BYTES)


def _pick(dim, pref, align=V7X_LANES):
    if dim <= pref:
        return dim
    t = pref - pref % align
    while t >= align:
        if dim % t == 0:
            return t
        t -= align
    return dim


def _mm_body(*refs, nk, k_axis, silu_in, has_bias, has_res, groups):
    it = iter(refs)
    x_ref, w_ref = next(it), next(it)
    b_ref = next(it) if has_bias else None
    r_ref = next(it) if has_res else None
    g_ref = next(it) if has_res else None
    o_ref, acc_ref = next(it), next(it)
    k = pl.program_id(k_axis)

    @pl.when(k == 0)
    def _():
        acc_ref[...] = jnp.zeros_like(acc_ref)

    x = x_ref[...]
    if silu_in:
        x = x.astype(F32)
        x = x * jax.nn.sigmoid(x)
    acc_ref[...] += jnp.dot(x.astype(BF16), w_ref[...].astype(BF16), preferred_element_type=F32)

    @pl.when(k == nk - 1)
    def _():
        r = acc_ref[...]
        if has_bias:
            r = r + b_ref[...]
        if has_res:
            tm, tn = r.shape
            r = (r.reshape(groups, tm // groups, tn) * g_ref[...]).reshape(tm, tn) + r_ref[...]
        o_ref[...] = r.astype(o_ref.dtype)


def _matmul(x, w, *, w_lead=(), n_off=0, n_out=None, bias=None, res=None, gate=None, gate_col=0,
            out_dtype=F32, tm=1024, tn=1024, tk=1024, name="matmul"):
    m, kdim = x.shape
    n = w.shape[-1] if n_out is None else n_out
    tm, tn, tk = _pick(m, tm, 8), _pick(n, tn), _pick(kdim, tk)
    nk = kdim // tk
    lead = tuple(w_lead)
    in_specs = [
        pl.BlockSpec((tm, tk), lambda i, j, k: (i, k)),
        pl.BlockSpec((None,) * len(lead) + (tk, tn), lambda i, j, k: lead + (k, j + n_off)),
    ]
    args = [x, w]
    if bias is not None:
        in_specs.append(pl.BlockSpec((1, tn), lambda i, j, k: (0, j)))
        args.append(bias.reshape(1, n))
    groups = 1
    if res is not None:
        rows = m // gate.shape[0]
        assert tm % rows == 0
        groups = tm // rows
        ncol = n // tn
        in_specs.append(pl.BlockSpec((tm, tn), lambda i, j, k: (i, j)))
        in_specs.append(pl.BlockSpec((groups, 1, tn), lambda i, j, k: (i, 0, gate_col * ncol + j)))
        args += [res, gate]
    body = functools.partial(_mm_body, nk=nk, k_axis=2, silu_in=False, has_bias=bias is not None,
                             has_res=res is not None, groups=groups)
    return pl.pallas_call(
        body,
        grid=(m // tm, n // tn, nk),
        in_specs=in_specs,
        out_specs=pl.BlockSpec((tm, tn), lambda i, j, k: (i, j)),
        out_shape=jax.ShapeDtypeStruct((m, n), out_dtype),
        scratch_shapes=[pltpu.VMEM((tm, tn), F32)],
        compiler_params=_params("parallel", "parallel", "arbitrary"),
        name=name,
    )(*args)


def _matmul_wbatched(x, w, bias=None, *, silu_in=False, out_dtype=F32, tm=1024, tn=1024, tk=512):
    m, kdim = x.shape
    nb, _, n = w.shape
    tm, tn, tk = _pick(m, tm, 8), _pick(n, tn), _pick(kdim, tk)
    nk = kdim // tk
    in_specs = [
        pl.BlockSpec((tm, tk), lambda b, i, j, k: (i, k)),
        pl.BlockSpec((None, tk, tn), lambda b, i, j, k: (b, k, j)),
    ]
    args = [x, w]
    if bias is not None:
        in_specs.append(pl.BlockSpec((None, 1, tn), lambda b, i, j, k: (b, 0, j)))
        args.append(bias.reshape(nb, 1, n))
    body = functools.partial(_mm_body, nk=nk, k_axis=3, silu_in=silu_in, has_bias=bias is not None,
                             has_res=False, groups=1)
    return pl.pallas_call(
        body,
        grid=(nb, m // tm, n // tn, nk),
        in_specs=in_specs,
        out_specs=pl.BlockSpec((None, tm, tn), lambda b, i, j, k: (b, i, j)),
        out_shape=jax.ShapeDtypeStruct((nb, m, n), out_dtype),
        scratch_shapes=[pltpu.VMEM((tm, tn), F32)],
        compiler_params=_params("parallel", "parallel", "parallel", "arbitrary"),
        name="adaln",
    )(*args)


def _modulated(x, g_ref, sh_ref, sc_ref):
    y = x * lax.rsqrt(jnp.mean(x * x, axis=-1, keepdims=True) + EPS) * g_ref[...]
    return y * (1.0 + sc_ref[0]) + sh_ref[0]


def _modulate_body(x_ref, g_ref, sh_ref, sc_ref, o_ref):
    o_ref[...] = _modulated(x_ref[...], g_ref, sh_ref, sc_ref).astype(o_ref.dtype)


def _mod_specs(rows, d, col):
    return [
        pl.BlockSpec((rows, d), lambda i: (i, 0)),
        pl.BlockSpec((1, d), lambda i: (0, 0)),
        pl.BlockSpec((1, 1, d), lambda i: (i, 0, col)),
        pl.BlockSpec((1, 1, d), lambda i: (i, 0, col + 1)),
    ]


def _modulate(x, g, modrows, col):
    t, d = x.shape
    rows = t // modrows.shape[0]
    return pl.pallas_call(
        _modulate_body,
        grid=(t // rows,),
        in_specs=_mod_specs(rows, d, col),
        out_specs=pl.BlockSpec((rows, d), lambda i: (i, 0)),
        out_shape=jax.ShapeDtypeStruct((t, d), BF16),
        compiler_params=_params("parallel"),
        name="modulate",
    )(x, g.reshape(1, d), modrows, modrows)


def _pack_bf16_pairs(x):
    n = x.shape[1] // 2
    lo = pltpu.bitcast(x[:, :n].astype(BF16).astype(F32), jnp.uint32)
    hi = pltpu.bitcast(x[:, n:].astype(BF16).astype(F32), jnp.uint32)
    return (hi & jnp.uint32(0xFFFF0000)) | (lo >> 16)


def _unpack_bf16_pairs(u):
    return pltpu.bitcast(u << 16, F32), pltpu.bitcast(u & jnp.uint32(0xFFFF0000), F32)


def _modulate_router_body(x_ref, g_ref, sh_ref, sc_ref, rw_ref, rb_ref, h_ref, ti_ref, tg_ref):
    h = _modulated(x_ref[...], g_ref, sh_ref, sc_ref)
    h_ref[...] = _pack_bf16_pairs(h)
    logits = jnp.dot(h, rw_ref[...], precision=lax.Precision.HIGHEST, preferred_element_type=F32) + rb_ref[...]
    lane = lax.broadcasted_iota(jnp.int32, logits.shape, 1).astype(F32)
    n_lane = float(logits.shape[1])
    vals, idxs = [], []
    for _ in range(TOP_K):
        m = jnp.max(logits, axis=-1, keepdims=True)
        idx = jnp.min(jnp.where(logits == m, lane, n_lane), axis=-1, keepdims=True)
        vals.append(m)
        idxs.append(idx)
        logits = jnp.where(lane == idx, -jnp.inf, logits)
    es = [jnp.exp(v - vals[0]) for v in vals]
    inv = 1.0 / functools.reduce(lambda a, b: a + b, es)
    ti = jnp.zeros(lane.shape, jnp.int32)
    tg = jnp.zeros(lane.shape, F32)
    for k in range(TOP_K):
        ti = jnp.where(lane == k, idxs[k].astype(jnp.int32), ti)
        tg = jnp.where(lane == k, es[k] * inv, tg)
    ti_ref[...] = ti
    tg_ref[...] = tg


def _modulate_router(x, g, modrows, col, rw, rb):
    t, d = x.shape
    rows = t // modrows.shape[0]
    e = rw.shape[1]
    rw_p = jnp.zeros((d, V7X_LANES), F32).at[:, :e].set(rw)
    rb_p = jnp.full((1, V7X_LANES), NEG_INF, F32).at[0, :e].set(rb)
    return pl.pallas_call(
        _modulate_router_body,
        grid=(t // rows,),
        in_specs=_mod_specs(rows, d, col) + [
            pl.BlockSpec((d, V7X_LANES), lambda i: (0, 0)),
            pl.BlockSpec((1, V7X_LANES), lambda i: (0, 0)),
        ],
        out_specs=[
            pl.BlockSpec((rows, d // 2), lambda i: (i, 0)),
            pl.BlockSpec((rows, V7X_LANES), lambda i: (i, 0)),
            pl.BlockSpec((rows, V7X_LANES), lambda i: (i, 0)),
        ],
        out_shape=[
            jax.ShapeDtypeStruct((t, d // 2), jnp.uint32),
            jax.ShapeDtypeStruct((t, V7X_LANES), jnp.int32),
            jax.ShapeDtypeStruct((t, V7X_LANES), F32),
        ],
        compiler_params=_params("parallel"),
        name="modulate_router",
    )(x, g.reshape(1, d), modrows, modrows, rw_p, rb_p)


def _dft_tables(n):
    j = jnp.arange(n, dtype=jnp.int32)
    ang = ((j[:, None] * j[None, :]) % n).astype(F32) * (2.0 * math.pi / n)
    s = 1.0 / math.sqrt(n)
    return jnp.cos(ang) * s, jnp.sin(ang) * s


def _fnet_channel_body(x_ref, w_ref, a_ref, b_ref):
    c = a_ref.shape[1]
    r = jnp.dot(x_ref[...], w_ref[...], preferred_element_type=F32)
    a_ref[...] = r[:, :c].astype(a_ref.dtype)
    b_ref[...] = r[:, c:].astype(b_ref.dtype)


def _fnet_channel_dft(h, tm=1024):
    t, d = h.shape
    c = d // FNET_GROUPS
    cos_c, sin_c = _dft_tables(c)
    cs = jnp.concatenate([cos_c, sin_c], axis=1).astype(BF16)
    tm = _pick(t, tm, 8)
    spec = pl.BlockSpec((tm, c), lambda i, g: (i, g))
    return pl.pallas_call(
        _fnet_channel_body,
        grid=(t // tm, FNET_GROUPS),
        in_specs=[spec, pl.BlockSpec((c, 2 * c), lambda i, g: (0, 0))],
        out_specs=[spec, spec],
        out_shape=[jax.ShapeDtypeStruct((t, d), BF16)] * 2,
        compiler_params=_params("parallel", "parallel"),
        name="fnet_channel_dft",
    )(h, cs)


def _fnet_seq_body(c_ref, s_ref, a_ref, b_ref, o_ref, acc_ref, *, nk):
    k = pl.program_id(3)

    @pl.when(k == 0)
    def _():
        acc_ref[...] = jnp.zeros_like(acc_ref)

    acc_ref[...] += (jnp.dot(c_ref[...], a_ref[...], preferred_element_type=F32)
                     + jnp.dot(s_ref[...], b_ref[...], preferred_element_type=F32))

    @pl.when(k == nk - 1)
    def _():
        o_ref[...] = acc_ref[...].astype(o_ref.dtype)


def _fnet_seq_dft(a, b, row_off, nb, seq, tm=1024, tn=1024, tk=1024):
    d = a.shape[1]
    cos_l, sin_l = _dft_tables(seq)
    cos_l, nsin_l = cos_l.astype(BF16), (-sin_l).astype(BF16)
    tm, tn, tk = _pick(seq, tm, 8), _pick(d, tn), _pick(seq, tk)
    nk, mt = seq // tk, seq // tm
    ro = row_off // tk
    tab = pl.BlockSpec((tm, tk), lambda bb, i, j, k: (i, k))
    src = pl.BlockSpec((tk, tn), lambda bb, i, j, k: (ro + bb * nk + k, j))
    return pl.pallas_call(
        functools.partial(_fnet_seq_body, nk=nk),
        grid=(nb, mt, d // tn, nk),
        in_specs=[tab, tab, src, src],
        out_specs=pl.BlockSpec((tm, tn), lambda bb, i, j, k: (bb * mt + i, j)),
        out_shape=jax.ShapeDtypeStruct((nb * seq, d), BF16),
        scratch_shapes=[pltpu.VMEM((tm, tn), F32)],
        compiler_params=_params("parallel", "parallel", "parallel", "arbitrary"),
        name="fnet_seq_dft",
    )(cos_l, nsin_l, a, b)


def _rope_tables(seq, rows_identity):
    half = HEAD_DIM // 2
    inv_freq = ROPE_THETA ** (-jnp.arange(0, half, 2, dtype=F32) / half)
    rows = seq // GRID_W
    row_pos = jnp.repeat(jnp.arange(rows, dtype=jnp.int32), GRID_W).astype(F32)
    col_pos = jnp.tile(jnp.arange(GRID_W, dtype=jnp.int32), rows).astype(F32)
    ar = row_pos[:, None] * inv_freq
    ac = col_pos[:, None] * inv_freq
    cos_t = jnp.concatenate([jnp.cos(ar), jnp.cos(ar), jnp.cos(ac), jnp.cos(ac)], axis=-1)
    sin_t = jnp.concatenate([-jnp.sin(ar), jnp.sin(ar), -jnp.sin(ac), jnp.sin(ac)], axis=-1)
    cos_t = jnp.concatenate([cos_t, jnp.ones((rows_identity, HEAD_DIM), F32)], axis=0)
    sin_t = jnp.concatenate([sin_t, jnp.zeros((rows_identity, HEAD_DIM), F32)], axis=0)
    return cos_t, sin_t


def _qknorm_rope_body(x_ref, g_ref, cos_ref, sin_ref, o_ref):
    cos_t, sin_t, gain = cos_ref[...], sin_ref[...], g_ref[...]
    lane = lax.broadcasted_iota(jnp.int32, cos_t.shape, 1)
    first = (lane % (HEAD_DIM // 2)) < (HEAD_DIM // 4)
    for h in range(x_ref.shape[1] // HEAD_DIM):
        sl = slice(h * HEAD_DIM, (h + 1) * HEAD_DIM)
        x = x_ref[:, sl]
        y = x * lax.rsqrt(jnp.mean(x * x, axis=-1, keepdims=True) + EPS) * gain
        partner = jnp.where(first, pltpu.roll(y, HEAD_DIM - HEAD_DIM // 4, 1), pltpu.roll(y, HEAD_DIM // 4, 1))
        o_ref[:, sl] = (y * cos_t + partner * sin_t).astype(o_ref.dtype)


def _qknorm_rope(qkv, gain, cos_t, sin_t, col_off, n_cols, rows, tab_index, out_dtype, tc=1024):
    t = qkv.shape[0]
    tc = _pick(n_cols, tc)
    off = col_off // tc
    return pl.pallas_call(
        _qknorm_rope_body,
        grid=(t // rows, n_cols // tc),
        in_specs=[
            pl.BlockSpec((rows, tc), lambda i, j: (i, off + j)),
            pl.BlockSpec((1, HEAD_DIM), lambda i, j: (0, 0)),
            pl.BlockSpec((rows, HEAD_DIM), lambda i, j: (tab_index(i), 0)),
            pl.BlockSpec((rows, HEAD_DIM), lambda i, j: (tab_index(i), 0)),
        ],
        out_specs=pl.BlockSpec((rows, tc), lambda i, j: (i, j)),
        out_shape=jax.ShapeDtypeStruct((t, n_cols), out_dtype),
        compiler_params=_params("parallel", "parallel"),
        name="qknorm_rope",
    )(qkv, gain.reshape(1, HEAD_DIM), cos_t, sin_t)


def _attn_body(sink_ref, q_ref, *refs, n_win):
    kw, vw = refs[:n_win], refs[n_win:2 * n_win]
    rest = refs[2 * n_win:]
    bias_ref = rest[0] if n_win else None
    kc_ref, vc_ref, o_ref = rest[-3:]
    g = pl.program_id(1)
    rep = q_ref.shape[1] // HEAD_DIM
    nq = q_ref.shape[0]
    scale = HEAD_DIM ** -0.5
    nt = (((1,), (1,)), ((), ()))
    q = jnp.concatenate([q_ref[:, r * HEAD_DIM:(r + 1) * HEAD_DIM] for r in range(rep)], axis=0).astype(BF16)
    sink = jnp.concatenate([jnp.full((nq, 1), sink_ref[g * rep + r], F32) for r in range(rep)], axis=0)
    s_c = lax.dot_general(q, kc_ref[...].astype(BF16), nt, preferred_element_type=F32) * scale
    m = jnp.maximum(jnp.max(s_c, axis=-1, keepdims=True), sink)
    if n_win:
        keys = jnp.concatenate([r[...].astype(BF16) for r in kw], axis=0)
        vals = jnp.concatenate([r[...].astype(BF16) for r in vw], axis=0)
        s_w = lax.dot_general(q, keys, nt, preferred_element_type=F32) * scale
        nk = s_w.shape[1]
        s_w = (s_w.reshape(rep, nq, nk) + bias_ref[...]).reshape(rep * nq, nk)
        m = jnp.maximum(m, jnp.max(s_w, axis=-1, keepdims=True))
    p_c = jnp.exp(s_c - m)
    denom = jnp.sum(p_c, axis=-1, keepdims=True) + jnp.exp(sink - m)
    if n_win:
        p_w = jnp.exp(s_w - m)
        denom = denom + jnp.sum(p_w, axis=-1, keepdims=True)
    inv = 1.0 / denom
    o = jnp.dot((p_c * inv).astype(BF16), vc_ref[...].astype(BF16), preferred_element_type=F32)
    if n_win:
        o = o + jnp.dot((p_w * inv).astype(BF16), vals, preferred_element_type=F32)
    for r in range(rep):
        o_ref[:, r * HEAD_DIM:(r + 1) * HEAD_DIM] = o[r * nq:(r + 1) * nq].astype(o_ref.dtype)


def _window_bias(seq):
    nblk = seq // BLOCK
    qi = jnp.arange(BLOCK, dtype=jnp.int32)[:, None]
    kj = jnp.arange(3 * BLOCK, dtype=jnp.int32)[None, :]

    def one(i):
        kpos = i * BLOCK - BLOCK + kj
        band = (jnp.abs(i * BLOCK + qi - kpos) <= WINDOW) & (kpos >= 0) & (kpos < seq)
        return jnp.where(band, 0.0, NEG_INF).astype(F32)

    return jnp.stack([one(0), one(min(1, nblk - 1)), one(nblk - 1)])


def _attention(qn, kn, qkv, v_col_off, sink, kc, vc, row_off, nb, seq, windowed):
    rep = N_HEADS // N_KV
    qw = rep * HEAD_DIM
    vo = v_col_off // HEAD_DIM
    if windowed:
        nblk = seq // BLOCK
        ro = row_off // BLOCK
        lc = kc.shape[1]

        def win(shift, col):
            return pl.BlockSpec(
                (BLOCK, HEAD_DIM),
                lambda b, g, i, s: (ro + b * nblk + jnp.clip(i + shift, 0, nblk - 1), col + g))

        in_specs = [pl.BlockSpec((BLOCK, qw), lambda b, g, i, s: (ro + b * nblk + i, g))]
        in_specs += [win(-1, 0), win(0, 0), win(1, 0), win(-1, vo), win(0, vo), win(1, vo)]
        in_specs.append(pl.BlockSpec(
            (None, BLOCK, 3 * BLOCK),
            lambda b, g, i, s: (jnp.where(i == 0, 0, jnp.where(i == nblk - 1, 2, 1)), 0, 0)))
        in_specs += [pl.BlockSpec((None, lc, HEAD_DIM), lambda b, g, i, s: (b, 0, g))] * 2
        args = [qn, kn, kn, kn, qkv, qkv, qkv, _window_bias(seq), kc, vc]
        grid = (nb, N_KV, nblk)
        out_spec = pl.BlockSpec((BLOCK, qw), lambda b, g, i, s: (b * nblk + i, g))
        body = functools.partial(_attn_body, n_win=3)
    else:
        ro = row_off // seq
        in_specs = [
            pl.BlockSpec((seq, qw), lambda b, g, i, s: (ro + b, g)),
            pl.BlockSpec((seq, HEAD_DIM), lambda b, g, i, s: (ro + b, g)),
            pl.BlockSpec((seq, HEAD_DIM), lambda b, g, i, s: (ro + b, vo + g)),
        ]
        args = [qn, kn, qkv]
        grid = (nb, N_KV, 1)
        out_spec = pl.BlockSpec((seq, qw), lambda b, g, i, s: (b, g))
        body = functools.partial(_attn_body, n_win=0)
    return pl.pallas_call(
        body,
        grid_spec=pltpu.PrefetchScalarGridSpec(
            num_scalar_prefetch=1, grid=grid, in_specs=in_specs, out_specs=out_spec),
        out_shape=jax.ShapeDtypeStruct((nb * seq, N_HEADS * HEAD_DIM), BF16),
        compiler_params=_params("parallel", "parallel", "parallel"),
        name="attention_window" if windowed else "attention_context",
    )(sink.astype(F32), *args)


def _conv_silu_body(x_ref, w_ref, b_ref, o_ref):
    x = x_ref[...]
    seq = x.shape[0]
    pad = SSD_CONV // 2
    t = lax.broadcasted_iota(jnp.int32, (seq, 1), 0)
    acc = b_ref[...] + x * w_ref[pad:pad + 1, :]
    for k in range(SSD_CONV):
        s = k - pad
        if s == 0:
            continue
        shifted = pltpu.roll(x, (-s) % seq, 0)
        acc = acc + jnp.where((t + s >= 0) & (t + s < seq), shifted, 0.0) * w_ref[k:k + 1, :]
    o_ref[...] = acc * jax.nn.sigmoid(acc)


def _conv_silu(zx, col_off, n_cols, w, b, row_off, nb, seq, tc=256):
    tc = _pick(n_cols, tc)
    co, ro = col_off // tc, row_off // seq
    return pl.pallas_call(
        _conv_silu_body,
        grid=(nb, n_cols // tc),
        in_specs=[
            pl.BlockSpec((seq, tc), lambda b, j: (ro + b, co + j)),
            pl.BlockSpec((SSD_CONV, tc), lambda b, j: (0, j)),
            pl.BlockSpec((1, tc), lambda b, j: (0, j)),
        ],
        out_specs=pl.BlockSpec((seq, tc), lambda b, j: (b, j)),
        out_shape=jax.ShapeDtypeStruct((nb * seq, n_cols), F32),
        compiler_params=_params("parallel", "parallel"),
        name="ssd_conv_silu",
    )(zx, w, b.reshape(1, n_cols))


def _softplus(x):
    return jnp.maximum(x, 0.0) + jnp.log(1.0 + jnp.exp(-jnp.abs(x)))


def _ssd_scan_body(x_ref, b_ref, c_ref, dt_ref, dtt_ref, db_ref, dbt_ref, al_ref, alt_ref, *refs,
                   nc, reverse, has_h0):
    if has_h0:
        h0_ref, y_ref, hf_ref, h_ref = refs
    else:
        y_ref, hf_ref, h_ref = refs
    c_idx = pl.program_id(2)
    q = x_ref.shape[0]
    hpg = dt_ref.shape[1]
    p = x_ref.shape[1] // hpg

    @pl.when(c_idx == 0)
    def _():
        h_ref[...] = h0_ref[...] if has_h0 else jnp.zeros_like(h_ref)

    hi = lax.Precision.HIGHEST
    ii = lax.broadcasted_iota(jnp.int32, (q, q), 0)
    jj = lax.broadcasted_iota(jnp.int32, (q, q), 1)
    incl = (jj >= ii) if reverse else (jj <= ii)
    inclf = incl.astype(F32)
    dt = _softplus(dt_ref[...] + db_ref[...])
    dtt = _softplus(dtt_ref[...] + dbt_ref[...])
    da = dt * (-jnp.exp(al_ref[...]))
    dat = dtt * (-jnp.exp(alt_ref[...]))
    acum = jnp.dot(inclf, da, precision=hi, preferred_element_type=F32)
    acumt = lax.dot_general(dat, inclf, (((1,), (1,)), ((), ())), precision=hi,
                            preferred_element_type=F32)
    total = jnp.sum(da, axis=0, keepdims=True)
    total_t = jnp.sum(dat, axis=1, keepdims=True)
    xb = x_ref[...]
    bm = b_ref[...].astype(BF16)
    cm = c_ref[...].astype(BF16)
    cb = lax.dot_general(cm, bm, (((1,), (1,)), ((), ())), preferred_element_type=F32)
    h = h_ref[...]
    ch = lax.dot_general(cm, h.astype(BF16), (((1,), (1,)), ((), ())), preferred_element_type=F32)
    lane = lax.broadcasted_iota(jnp.int32, (q, 2 * p), 1)
    decay = []
    for r in range(0, hpg, 2):
        sl = slice(r * p, (r + 2) * p)
        mats, ex = [], []
        for rr in (r, r + 1):
            ai = jnp.broadcast_to(acum[:, rr:rr + 1], (q, q))
            lmat = jnp.exp(jnp.where(incl, ai - acumt[rr:rr + 1, :], -jnp.inf))
            mats.append((cb * lmat * dtt[rr:rr + 1, :]).astype(BF16))
            ex.append(jnp.exp(ai[:, :2 * p]))
            decay.append(jnp.broadcast_to(jnp.exp(total[:, rr:rr + 1]), (p, 1)))
        x2 = xb[:, sl]
        rhs = jnp.concatenate([jnp.where(lane < p, x2, 0.0), jnp.where(lane >= p, x2, 0.0)], axis=0).astype(BF16)
        intra = jnp.dot(jnp.concatenate(mats, axis=1), rhs, preferred_element_type=F32)
        y_ref[:, sl] = intra + ch[:, sl] * jnp.where(lane < p, ex[0], ex[1])
    w_t = jnp.exp(total_t - acumt) * dtt
    xwt = (xb.T.reshape(hpg, p, q) * w_t[:, None, :]).reshape(hpg * p, q).astype(BF16)
    h_new = h * jnp.concatenate(decay, axis=0) + jnp.dot(xwt, bm, preferred_element_type=F32)
    h_ref[...] = h_new

    @pl.when(c_idx == nc - 1)
    def _():
        hf_ref[...] = h_new


def _ssd_scan(xbc, dt_r, dt_t, dt_bias, a_log, h0, direction, nb, seq, inner):
    nc = seq // SSD_CHUNK
    hpg = inner // SSD_HEADDIM // SSD_GROUPS
    gw = hpg * SSD_HEADDIM
    reverse = direction == 1
    bo = inner // SSD_STATE
    co = bo + SSD_GROUPS

    def tok(b, c):
        return b * nc + (nc - 1 - c if reverse else c)

    db = dt_bias.reshape(2, SSD_GROUPS, 1, hpg)
    al = a_log.reshape(2, SSD_GROUPS, 1, hpg)
    in_specs = [
        pl.BlockSpec((SSD_CHUNK, gw), lambda b, g, c: (tok(b, c), g)),
        pl.BlockSpec((SSD_CHUNK, SSD_STATE), lambda b, g, c: (tok(b, c), bo + g)),
        pl.BlockSpec((SSD_CHUNK, SSD_STATE), lambda b, g, c: (tok(b, c), co + g)),
        pl.BlockSpec((None, None, SSD_CHUNK, hpg), lambda b, g, c: (direction, g, tok(b, c), 0)),
        pl.BlockSpec((None, None, hpg, SSD_CHUNK), lambda b, g, c: (direction, g, 0, tok(b, c))),
        pl.BlockSpec((None, None, 1, hpg), lambda b, g, c: (direction, g, 0, 0)),
        pl.BlockSpec((None, None, hpg, 1), lambda b, g, c: (direction, g, 0, 0)),
        pl.BlockSpec((None, None, 1, hpg), lambda b, g, c: (direction, g, 0, 0)),
        pl.BlockSpec((None, None, hpg, 1), lambda b, g, c: (direction, g, 0, 0)),
    ]
    args = [xbc, xbc, xbc, dt_r, dt_t, db, db.reshape(2, SSD_GROUPS, hpg, 1), al, al.reshape(2, SSD_GROUPS, hpg, 1)]
    if h0 is not None:
        in_specs.append(pl.BlockSpec((None, None, gw, SSD_STATE), lambda b, g, c: (b, g, 0, 0)))
        args.append(h0)
    body = functools.partial(_ssd_scan_body, nc=nc, reverse=reverse, has_h0=h0 is not None)
    return pl.pallas_call(
        body,
        grid=(nb, SSD_GROUPS, nc),
        in_specs=in_specs,
        out_specs=[
            pl.BlockSpec((SSD_CHUNK, gw), lambda b, g, c: (tok(b, c), g)),
            pl.BlockSpec((None, None, gw, SSD_STATE), lambda b, g, c: (b, g, 0, 0)),
        ],
        out_shape=[
            jax.ShapeDtypeStruct((nb * seq, inner), F32),
            jax.ShapeDtypeStruct((nb, SSD_GROUPS, gw, SSD_STATE), F32),
        ],
        scratch_shapes=[pltpu.VMEM((gw, SSD_STATE), F32)],
        compiler_params=_params("parallel", "parallel", "arbitrary"),
        name="ssd_scan_bwd" if reverse else "ssd_scan_fwd",
    )(*args)


def _gated_norm_body(yf_ref, yb_ref, xs_ref, z_ref, d_ref, g_ref, o_ref):
    y = yf_ref[...] + yb_ref[...] + d_ref[...] * xs_ref[...]
    z = z_ref[...]
    u = y * (z * jax.nn.sigmoid(z))
    u = u * lax.rsqrt(jnp.mean(u * u, axis=-1, keepdims=True) + EPS)
    o_ref[...] = (u * g_ref[...]).astype(o_ref.dtype)


def _gated_norm(yf, yb, xbc, zx, z_row_off, d_cols, g, inner, tr=256):
    t = yf.shape[0]
    gw = inner // SSD_GROUPS
    tr = _pick(t, tr, 8)
    zo = z_row_off // tr
    spec = pl.BlockSpec((tr, gw), lambda i, j: (i, j))
    vec = pl.BlockSpec((1, gw), lambda i, j: (0, j))
    return pl.pallas_call(
        _gated_norm_body,
        grid=(t // tr, SSD_GROUPS),
        in_specs=[spec, spec, spec, pl.BlockSpec((tr, gw), lambda i, j: (zo + i, j)), vec, vec],
        out_specs=spec,
        out_shape=jax.ShapeDtypeStruct((t, inner), BF16),
        compiler_params=_params("parallel", "parallel"),
        name="ssd_gated_norm",
    )(yf, yb, xbc, zx, d_cols.reshape(1, inner), g.reshape(1, inner))


def _moe_route(ti, n_experts, bm):
    t = ti.shape[0]
    n_slots = t * TOP_K
    flat_e = ti.reshape(-1)
    onehot = (flat_e[:, None] == jnp.arange(n_experts, dtype=jnp.int32)[None, :]).astype(jnp.int32)
    csum = jnp.cumsum(onehot, axis=0)
    counts = csum[-1]
    rank = jnp.take_along_axis(csum, flat_e[:, None], axis=1)[:, 0] - 1
    padded = (counts + bm - 1) // bm * bm
    pad_end = jnp.cumsum(padded)
    dest = (pad_end - padded)[flat_e] + rank
    n_blocks = n_slots // bm + n_experts
    starts = jnp.arange(n_blocks, dtype=jnp.int32) * bm
    n_used = pad_end[-1] // bm
    block_exp = jnp.minimum(jnp.sum(starts[:, None] >= pad_end[None, :], axis=1), n_experts - 1).astype(jnp.int32)
    block_exp = jnp.where(jnp.arange(n_blocks) < n_used, block_exp, block_exp[jnp.maximum(n_used - 1, 0)])
    return n_blocks * bm, block_exp, n_used.astype(jnp.int32).reshape(1), dest.reshape(t, TOP_K)


def _row_copy(src_hbm, row, dst, sem):
    return pltpu.make_async_copy(src_hbm.at[pl.ds(row, 1)], dst, sem)


def _burst_rows(n, rows):
    per_sub = rows // V7X_SUBLANES // DMA_UNROLL
    base = (n % per_sub) * (DMA_UNROLL * V7X_SUBLANES) + n // per_sub
    return [base + u * V7X_SUBLANES for u in range(DMA_UNROLL)]


def _dispatch_rows_body(dest_ref, hp_ref, init_hbm, out_hbm, sem):
    del init_hbm
    tr = hp_ref.shape[0]

    def copy(r, slot):
        return pltpu.make_async_copy(hp_ref.at[pl.ds(r, 1)], out_hbm.at[pl.ds(slot, 1)], sem)

    def burst(b, carry):
        for r in _burst_rows(b, tr):
            for k in range(TOP_K):
                copy(r, dest_ref[0, 0, r * TOP_K + k]).start()
        return carry

    def wait(r, carry):
        for k in range(TOP_K):
            copy(r, 0).wait()
        return carry

    lax.fori_loop(0, tr // DMA_UNROLL, burst, 0)
    lax.fori_loop(0, tr, wait, 0, unroll=DMA_UNROLL // TOP_K)


def _dispatch_rows(hp, dest, init, tr=256):
    t, half = hp.shape
    cap = init.shape[0]
    tr = _pick(t, tr, 8)
    assert tr % (V7X_SUBLANES * DMA_UNROLL) == 0
    return pl.pallas_call(
        _dispatch_rows_body,
        grid=(t // tr,),
        in_specs=[
            pl.BlockSpec((1, 1, tr * TOP_K), lambda i: (i, 0, 0), memory_space=pltpu.SMEM),
            pl.BlockSpec((tr, half), lambda i: (i, 0)),
            pl.BlockSpec(memory_space=pl.ANY),
        ],
        out_specs=pl.BlockSpec(memory_space=pl.ANY),
        out_shape=jax.ShapeDtypeStruct((cap, half), jnp.uint32),
        scratch_shapes=[pltpu.SemaphoreType.DMA(())],
        input_output_aliases={2: 0},
        compiler_params=_params("arbitrary"),
        name="moe_dispatch",
    )(dest.reshape(t // tr, 1, tr * TOP_K), hp, init)


def _expert_changed(be_ref, m):
    return (m == 0) | (be_ref[m] != be_ref[jnp.maximum(m - 1, 0)])


def _moe_up_body(be_ref, nu_ref, x_ref, wg_ref, wu_ref, bg_ref, bu_ref, o_ref, wg_s, wu_s):
    m = pl.program_id(1)

    @pl.when(m < nu_ref[0])
    def _():
        @pl.when(_expert_changed(be_ref, m))
        def _():
            wg_s[...] = wg_ref[...].astype(BF16)
            wu_s[...] = wu_ref[...].astype(BF16)

        half = x_ref.shape[1]
        lo, hi = _unpack_bf16_pairs(x_ref[...])
        lo, hi = lo.astype(BF16), hi.astype(BF16)
        g = (jnp.dot(lo, wg_s[:half, :], preferred_element_type=F32)
             + jnp.dot(hi, wg_s[half:, :], preferred_element_type=F32) + bg_ref[...])
        u = (jnp.dot(lo, wu_s[:half, :], preferred_element_type=F32)
             + jnp.dot(hi, wu_s[half:, :], preferred_element_type=F32) + bu_ref[...])
        g = jnp.minimum(g, SWIGLU_LIMIT)
        u = jnp.clip(u, -SWIGLU_LIMIT, SWIGLU_LIMIT)
        o_ref[...] = ((u + 1.0) * (g * jax.nn.sigmoid(SWIGLU_ALPHA * g))).astype(o_ref.dtype)

    @pl.when(m >= nu_ref[0])
    def _():
        o_ref[...] = jnp.zeros_like(o_ref)


def _moe_up(xs, w, b, layer, block_exp, n_used, bm, tn=512):
    cap, half = xs.shape
    d = 2 * half
    e, f2 = b.shape[1], b.shape[2]
    f = f2 // 2
    tn = _pick(f, tn)
    nf = f // tn
    b4 = b.reshape(b.shape[0], e, 1, f2)
    wspec = lambda off: pl.BlockSpec((None, None, d, tn), lambda j, m, be, nu: (layer, be[m], 0, off + j))
    bspec = lambda off: pl.BlockSpec((None, None, 1, tn), lambda j, m, be, nu: (layer, be[m], 0, off + j))
    return pl.pallas_call(
        _moe_up_body,
        grid_spec=pltpu.PrefetchScalarGridSpec(
            num_scalar_prefetch=2,
            grid=(nf, cap // bm),
            in_specs=[pl.BlockSpec((bm, half), lambda j, m, be, nu: (m, 0)), wspec(0), wspec(nf), bspec(0), bspec(nf)],
            out_specs=pl.BlockSpec((bm, tn), lambda j, m, be, nu: (m, j)),
            scratch_shapes=[pltpu.VMEM((d, tn), BF16), pltpu.VMEM((d, tn), BF16)],
        ),
        out_shape=jax.ShapeDtypeStruct((cap, f), BF16),
        compiler_params=_params("parallel", "arbitrary"),
        name="moe_up",
    )(block_exp, n_used, xs, w, w, b4, b4)


def _moe_down_body(be_ref, nu_ref, a_ref, w_ref, b_ref, o_ref, w_s):
    m = pl.program_id(1)

    @pl.when(m < nu_ref[0])
    def _():
        @pl.when(_expert_changed(be_ref, m))
        def _():
            w_s[...] = w_ref[...].astype(BF16)

        o_ref[...] = _pack_bf16_pairs(jnp.dot(a_ref[...], w_s[...], preferred_element_type=F32) + b_ref[...])

    @pl.when(m >= nu_ref[0])
    def _():
        o_ref[...] = jnp.zeros_like(o_ref)


def _moe_down(act, w, b, layer, block_exp, n_used, bm, tn):
    cap, f = act.shape
    e, d = b.shape[1], b.shape[2]
    b4 = b.reshape(b.shape[0], e, 1, d)
    return pl.pallas_call(
        _moe_down_body,
        grid_spec=pltpu.PrefetchScalarGridSpec(
            num_scalar_prefetch=2,
            grid=(d // tn, cap // bm),
            in_specs=[
                pl.BlockSpec((bm, f), lambda j, m, be, nu: (m, 0)),
                pl.BlockSpec((None, None, f, tn), lambda j, m, be, nu: (layer, be[m], 0, j)),
                pl.BlockSpec((None, None, 1, tn), lambda j, m, be, nu: (layer, be[m], 0, j)),
            ],
            out_specs=pl.BlockSpec((bm, tn // 2), lambda j, m, be, nu: (m, j)),
            scratch_shapes=[pltpu.VMEM((f, tn), BF16)],
        ),
        out_shape=jax.ShapeDtypeStruct((cap, d // 2), jnp.uint32),
        compiler_params=_params("parallel", "arbitrary"),
        name="moe_down",
    )(block_exp, n_used, act, w, b4)


def _moe_combine_body(pos_ref, nxt_ref, x_ref, tg_ref, g2_ref, y_hbm, o_ref, buf, sem, *, tn):
    tr = x_ref.shape[0]
    i, n = pl.program_id(0), pl.num_programs(0)
    slot = i % 2

    def start_block(idx_ref, s):
        def burst(b, carry):
            for r in _burst_rows(b, tr):
                for k in range(TOP_K):
                    _row_copy(y_hbm, idx_ref[0, 0, r * TOP_K + k], buf.at[s, k, pl.ds(r, 1)], sem.at[s]).start()
            return carry

        lax.fori_loop(0, tr // DMA_UNROLL, burst, 0)

    @pl.when(i == 0)
    def _():
        start_block(pos_ref, 0)

    @pl.when(i + 1 < n)
    def _():
        start_block(nxt_ref, 1 - slot)

    def wait(r, carry):
        for k in range(TOP_K):
            _row_copy(y_hbm, 0, buf.at[slot, k, pl.ds(r, 1)], sem.at[slot]).wait()
        return carry

    lax.fori_loop(0, tr, wait, 0, unroll=DMA_UNROLL // TOP_K)
    tg = tg_ref[...]
    pw = tn // 2
    for j in range(x_ref.shape[1] // tn):
        lo = hi = None
        for k in range(TOP_K):
            lk, hk = _unpack_bf16_pairs(buf[slot, k, :, j * pw:(j + 1) * pw])
            gk = tg[:, k:k + 1]
            lo = lk * gk if lo is None else lo + lk * gk
            hi = hk * gk if hi is None else hi + hk * gk
        for part, y in ((slice(j * tn, j * tn + pw), lo), (slice(j * tn + pw, (j + 1) * tn), hi)):
            o_ref[:, part] = x_ref[:, part] + g2_ref[0, :, part] * y


def _moe_combine(x, y_slots, pos, tg, modrows, gate_col, tn, tr=128):
    t, d = x.shape
    rows = t // modrows.shape[0]
    tr = _pick(rows, tr, 8)
    per = rows // tr
    nblk = t // tr
    assert tr % (V7X_SUBLANES * DMA_UNROLL) == 0
    pos3 = pos.reshape(nblk, 1, tr * TOP_K)
    return pl.pallas_call(
        functools.partial(_moe_combine_body, tn=tn),
        grid=(nblk,),
        in_specs=[
            pl.BlockSpec((1, 1, tr * TOP_K), lambda i: (i, 0, 0), memory_space=pltpu.SMEM),
            pl.BlockSpec((1, 1, tr * TOP_K), lambda i: (jnp.minimum(i + 1, nblk - 1), 0, 0), memory_space=pltpu.SMEM),
            pl.BlockSpec((tr, d), lambda i: (i, 0)),
            pl.BlockSpec((tr, V7X_LANES), lambda i: (i, 0)),
            pl.BlockSpec((1, 1, d), lambda i: (i // per, 0, gate_col)),
            pl.BlockSpec(memory_space=pl.ANY),
        ],
        out_specs=pl.BlockSpec((tr, d), lambda i: (i, 0)),
        out_shape=jax.ShapeDtypeStruct((t, d), F32),
        scratch_shapes=[pltpu.VMEM((2, TOP_K, tr, d // 2), jnp.uint32), pltpu.SemaphoreType.DMA((2,))],
        compiler_params=_params("arbitrary"),
        name="moe_combine",
    )(pos3, pos3, x, tg, modrows, y_slots)


def _moe_ffn(x, g, modrows, layer, router_w, router_b, w_gate_up, b_gate_up, w_down, b_down, slots):
    n_experts = router_w.shape[-1]
    bm = MOE_BLOCK_ROWS
    tn_down = _pick(x.shape[1], MOE_DOWN_COLS)
    hp, ti, tg = _modulate_router(x, g, modrows, 3, router_w[layer], router_b[layer])
    cap, block_exp, n_used, pos = _moe_route(ti[:, :TOP_K], n_experts, bm)
    if slots is None:
        slots = jnp.zeros((cap, hp.shape[1]), jnp.uint32)
    xs = _dispatch_rows(hp, pos, slots)
    act = _moe_up(xs, w_gate_up, b_gate_up, layer, block_exp, n_used, bm)
    y_slots = _moe_down(act, w_down, b_down, layer, block_exp, n_used, bm, tn_down)
    return _moe_combine(x, y_slots, pos, tg, modrows, 5, tn_down), xs


def kernel(x_prompt, x_sample, cache_attn_k, cache_attn_v, state_ssd_fwd, state_ssd_bwd, c, c_ctx, w_ada, b_ada, norm1_g, norm2_g, fnet_w_out, fnet_b_out, attn_w_qkv, attn_q_gain, attn_k_gain, attn_sink, attn_w_out, ssd_w_in, ssd_conv_w, ssd_conv_b, ssd_dt_bias, ssd_a_log, ssd_d, ssd_norm_g, ssd_w_out, moe_router_w, moe_router_b, moe_w_gate_up, moe_b_gate_up, moe_w_down, moe_b_down):
    bp, lp, d = x_prompt.shape
    bs, ls, _ = x_sample.shape
    tp, ts = bp * lp, bs * ls
    t = tp + ts
    depth = w_ada.shape[0]
    rows = math.gcd(lp, ls)
    x = jnp.concatenate([x_prompt.reshape(tp, d), x_sample.reshape(ts, d)], axis=0)

    assert 1 + bs <= COND_ROWS
    cond = jnp.zeros((COND_ROWS, d), F32).at[0].set(c_ctx).at[1:1 + bs].set(c)
    mod = _matmul_wbatched(cond, w_ada, b_ada, silu_in=True, tm=COND_ROWS, tn=1024, tk=2048)
    group_cond = jnp.concatenate([jnp.zeros((tp // rows,), jnp.int32),
                                  1 + jnp.arange(ts // rows, dtype=jnp.int32) // (ls // rows)])

    q_dim, kv_dim = N_HEADS * HEAD_DIM, N_KV * HEAD_DIM
    fnet_w = fnet_w_out.astype(BF16)
    new_k, new_v, new_f, new_b = [], [], [], []
    slots = None
    for l in range(depth):
        kind, j = l % N_MIXERS, l // N_MIXERS
        modrows = mod[l][group_cond].reshape(t // rows, 1, 6 * d)
        h = _modulate(x, norm1_g[l], modrows, 0)
        if kind == 0:
            fa, fb = _fnet_channel_dft(h)
            mixed = jnp.concatenate([_fnet_seq_dft(fa, fb, 0, bp, lp), _fnet_seq_dft(fa, fb, tp, bs, ls)], axis=0)
            x = _matmul(mixed, fnet_w, w_lead=(j,), bias=fnet_b_out[j], res=x, gate=modrows, gate_col=2,
                        name="fnet_out")
        elif kind == 1:
            qkv = _matmul(h, attn_w_qkv.astype(BF16), w_lead=(j,), tm=2048, name="attn_qkv")
            cos_t, sin_t = _rope_tables(ls, rows)
            n_tab = ls // rows
            tab = lambda i: jnp.where(i < tp // rows, n_tab, (i - tp // rows) % n_tab)
            qn = _qknorm_rope(qkv, attn_q_gain[j], cos_t, sin_t, 0, q_dim, rows, tab, BF16)
            kn = _qknorm_rope(qkv, attn_k_gain[j], cos_t, sin_t, q_dim, kv_dim, rows, tab, F32)
            new_k.append(kn[:tp].reshape(bp, lp, N_KV, HEAD_DIM))
            new_v.append(qkv[:tp, q_dim + kv_dim:].reshape(bp, lp, N_KV, HEAD_DIM))
            op = _attention(qn, kn, qkv, q_dim + kv_dim, attn_sink[j], None, None, 0, bp, lp, False)
            kc = cache_attn_k[:, j].reshape(bs, -1, kv_dim)
            vc = cache_attn_v[:, j].reshape(bs, -1, kv_dim)
            os_ = _attention(qn, kn, qkv, q_dim + kv_dim, attn_sink[j], kc, vc, tp, bs, ls, True)
            o = jnp.concatenate([op, os_], axis=0)
            x = _matmul(o, attn_w_out.astype(BF16), w_lead=(j,), res=x, gate=modrows, gate_col=2, name="attn_out")
        else:
            inner = ssd_w_out.shape[1]
            heads = inner // SSD_HEADDIM
            hpg = heads // SSD_GROUPS
            gn = SSD_GROUPS * SSD_STATE
            n_zx = 2 * inner + 2 * gn
            tn_zx = _pick(n_zx, 1024)
            w_in = ssd_w_in.astype(BF16)
            zx = _matmul(h, w_in, w_lead=(j,), n_out=n_zx, tm=2048, tn=tn_zx, name="ssd_in")
            dt_raw = _matmul(h, w_in, w_lead=(j,), n_off=n_zx // (2 * heads), n_out=2 * heads, tn=2 * heads,
                             name="ssd_in_dt")
            dt_r = dt_raw.reshape(t, 2, SSD_GROUPS, hpg).transpose(1, 2, 0, 3)
            dt_t = dt_raw.reshape(t, 2, SSD_GROUPS, hpg).transpose(1, 2, 3, 0)
            d_cols = jnp.repeat(ssd_d[j], SSD_HEADDIM)
            ys = []
            for (row_off, nb, seq, s_f, s_b) in ((0, bp, lp, None, None),
                                                 (tp, bs, ls, state_ssd_fwd[:, j], state_ssd_bwd[:, j])):
                xbc = _conv_silu(zx, inner, inner + 2 * gn, ssd_conv_w[j], ssd_conv_b[j], row_off, nb, seq)
                sl = slice(row_off, row_off + nb * seq)
                h0 = [None if s is None else s.reshape(nb, SSD_GROUPS, hpg * SSD_HEADDIM, SSD_STATE) for s in (s_f, s_b)]
                yf, hf = _ssd_scan(xbc, dt_r[:, :, sl], dt_t[:, :, :, sl], ssd_dt_bias[j], ssd_a_log[j], h0[0], 0, nb, seq, inner)
                yb, hb = _ssd_scan(xbc, dt_r[:, :, sl], dt_t[:, :, :, sl], ssd_dt_bias[j], ssd_a_log[j], h0[1], 1, nb, seq, inner)
                if s_f is None:
                    new_f.append(hf.reshape(nb, heads, SSD_HEADDIM, SSD_STATE))
                    new_b.append(hb.reshape(nb, heads, SSD_HEADDIM, SSD_STATE))
                ys.append(_gated_norm(yf, yb, xbc, zx, row_off, d_cols, ssd_norm_g[j], inner))
            x = _matmul(jnp.concatenate(ys, axis=0), ssd_w_out.astype(BF16), w_lead=(j,), res=x, gate=modrows,
                        gate_col=2, name="ssd_out")
        x, slots = _moe_ffn(x, norm2_g[l], modrows, l, moe_router_w, moe_router_b,
                            moe_w_gate_up, moe_b_gate_up, moe_w_down, moe_b_down, slots)

    return (x[:tp].reshape(bp, lp, d), x[tp:].reshape(bs, ls, d),
            jnp.stack(new_k, axis=1), jnp.stack(new_v, axis=1),
            jnp.stack(new_f, axis=1), jnp.stack(new_b, axis=1))
```
